```python
import math
import jax, jax.numpy as jnp
from jax import lax
import numpy as np

D_MODEL = 1024
BATCH = 2
SEQ = 8192
DEPTH = 1
DEC_BATCH = 32
DEC_SEQ = 64
PAST_LEN = 4096

CHUNK = 64
CONV_W = 4
D_RNN = 1024
RG_BLOCKS = 8
RG_BW = D_RNN // RG_BLOCKS
RG_C = 8.0
D_ML = 2 * D_MODEL
H_ML = 8
DH_IN = D_ML // H_ML
DK = 128
DV = 256
D_FF = 4 * D_MODEL
EPS = 1e-6
IN_SIZES = (D_RNN, D_RNN, D_ML, D_ML, H_ML, H_ML, D_MODEL, D_MODEL)
D_IN_TOTAL = 2 * D_RNN + 2 * D_ML + 2 * H_ML + 2 * D_MODEL

kernel_name = 'hawk_mlstm_parallel_streaming_step'


def rmsnorm(x, g):
    xf = x.astype(jnp.float32)
    y = xf * lax.rsqrt(jnp.mean(xf * xf, axis=-1, keepdims=True) + EPS)
    return (y * g.astype(jnp.float32)).astype(x.dtype)


def causal_conv(x, prev, w, b):
    S = x.shape[1]
    xp = jnp.concatenate([prev.astype(x.dtype), x], axis=1)
    y = b + sum(xp[:, k:k + S] * w[k] for k in range(CONV_W))
    return y, xp[:, -(CONV_W - 1):]


def rglru(xc, h0, wa, ba, wx, bx, lam):
    B, S, _ = xc.shape
    xf = xc.astype(jnp.float32)
    xb = xf.reshape(B, S, RG_BLOCKS, RG_BW)
    r = jax.nn.sigmoid(jnp.einsum('bsnc,ncd->bsnd', xb, wa.astype(jnp.float32)).reshape(B, S, D_RNN) + ba)
    i = jax.nn.sigmoid(jnp.einsum('bsnc,ncd->bsnd', xb, wx.astype(jnp.float32)).reshape(B, S, D_RNN) + bx)
    log_a = -RG_C * r * jax.nn.softplus(-lam.astype(jnp.float32))
    a = jnp.exp(log_a)
    gain = jnp.sqrt(-jnp.expm1(2.0 * log_a))
    bterm = gain * (i * xf)
    bterm = bterm.at[:, 0].add(a[:, 0] * h0.astype(jnp.float32))

    def combine(left, right):
        a1, b1 = left
        a2, b2 = right
        return a1 * a2, a2 * b1 + b2

    _, h = lax.associative_scan(combine, (a, bterm), axis=1)
    return h, h[:, -1]


def mlstm_chunkwise(q, k, v, ig, lf, C0, n0, m0):
    B, S, H, _ = q.shape
    L = CHUNK if S % CHUNK == 0 else S
    nc = S // L
    mask = jnp.tril(jnp.ones((L, L), dtype=bool))

    def chunks(t):
        return jnp.moveaxis(t.reshape((B, nc, L) + t.shape[2:]), 1, 0)

    def step(carry, xs):
        C, n, m = carry
        qc, kc, vc, ic, fc = xs
        bcum = jnp.cumsum(fc, axis=1)
        bl = bcum[:, -1]
        dmat = bcum[:, :, None, :] - bcum[:, None, :, :] + ic[:, None, :, :]
        dmat = jnp.where(mask[None, :, :, None], dmat, -jnp.inf)
        inter = bcum + m[:, None, :]
        m_t = jnp.maximum(inter, jnp.max(dmat, axis=2))
        w_intra = jnp.exp(dmat - m_t[:, :, None, :])
        w_inter = jnp.exp(inter - m_t)
        s = jnp.einsum('bthk,bshk->btsh', qc, kc) * w_intra
        num = jnp.einsum('btsh,bshv->bthv', s, vc) + w_inter[..., None] * jnp.einsum('bthk,bhvk->bthv', qc, C)
        den = jnp.sum(s, axis=2) + w_inter * jnp.einsum('bthk,bhk->bth', qc, n)
        h = num / jnp.maximum(jnp.abs(den), jnp.exp(-m_t))[..., None]
        g = bl[:, None, :] - bcum + ic
        m_new = jnp.maximum(bl + m, jnp.max(g, axis=1))
        wk = jnp.exp(g - m_new[:, None, :])
        decay = jnp.exp(bl + m - m_new)
        C_new = decay[..., None, None] * C + jnp.einsum('bsh,bshv,bshk->bhvk', wk, vc, kc)
        n_new = decay[..., None] * n + jnp.einsum('bsh,bshk->bhk', wk, kc)
        return (C_new, n_new, m_new), h

    carry0 = (C0.astype(jnp.float32), n0.astype(jnp.float32), m0.astype(jnp.float32))
    xs = tuple(chunks(t.astype(jnp.float32)) for t in (q, k, v, ig, lf))
    (C, n, m), h = lax.scan(step, carry0, xs)
    h = jnp.moveaxis(h, 0, 1).reshape(B, S, H, DV)
    return h, C, n, m


def block(x, conv_r, h_r, conv_m, C, n, m, norm_mix, w_in, b_if, cw_r, cb_r, wa, ba, wx, bx, lam,
          cw_m, cb_m, wq, wk, wv, wpa, wpb, w_out, norm_mlp, w_up, w_down):
    B, S, _ = x.shape
    u = rmsnorm(x, norm_mix)
    proj = u @ w_in
    parts = []
    off = 0
    for sz in IN_SIZES:
        parts.append(proj[..., off:off + sz])
        off += sz
    xr, gr, xm, zm, pi, pf, ga, gb = parts
    xr_c, conv_r_new = causal_conv(xr, conv_r, cw_r, cb_r)
    h_seq, h_last = rglru(xr_c, h_r, wa, ba, wx, bx, lam)
    y_a = h_seq.astype(x.dtype) * jax.nn.gelu(gr)
    xm_c, conv_m_new = causal_conv(xm, conv_m, cw_m, cb_m)
    xm_c = jax.nn.silu(xm_c).reshape(B, S, H_ML, DH_IN)
    xv = xm.reshape(B, S, H_ML, DH_IN)
    q = jnp.einsum('bshc,hck->bshk', xm_c, wq) * (DK ** -0.5)
    k = jnp.einsum('bshc,hck->bshk', xm_c, wk)
    v = jnp.einsum('bshc,hcv->bshv', xv, wv)
    ig = (pi + b_if[:H_ML]).astype(jnp.float32)
    lf = jax.nn.log_sigmoid((pf + b_if[H_ML:]).astype(jnp.float32))
    h_m, C_new, n_new, m_new = mlstm_chunkwise(q, k, v, ig, lf, C, n, m)
    y_b = jax.nn.sigmoid(zm) * h_m.reshape(B, S, D_ML).astype(x.dtype)
    mix = jax.nn.sigmoid(ga) * (y_a @ wpa) + jax.nn.sigmoid(gb) * (y_b @ wpb)
    x = x + mix @ w_out
    hmlp = jax.nn.relu(rmsnorm(x, norm_mlp) @ w_up)
    x = x + (hmlp * hmlp) @ w_down
    return x, conv_r_new, h_last, conv_m_new, C_new, n_new, m_new


def setup_inputs(seed: int = 0) -> dict:
    key = jax.random.key(seed)
    ks = jax.random.split(key, 32)
    f32 = jnp.float32
    nrm = lambda k, shape, s: jax.random.normal(k, shape, f32) * s
    u_a = jax.random.uniform(ks[0], (DEPTH, D_RNN), f32, 0.9, 0.999)
    a_base = u_a ** (1.0 / RG_C)
    lam = jnp.log(a_base) - jnp.log1p(-a_base)
    b_if = jnp.concatenate([nrm(ks[1], (DEPTH, H_ML), 0.1),
                            3.0 + nrm(ks[2], (DEPTH, H_ML), 0.3)], axis=-1)
    return {
        'x_prompt': nrm(ks[3], (BATCH, SEQ, D_MODEL), 1.0),
        'x_sample': nrm(ks[4], (DEC_BATCH, DEC_SEQ, D_MODEL), 1.0),
        'state_rglru_conv': nrm(ks[5], (DEPTH, DEC_BATCH, CONV_W - 1, D_RNN), 1.0),
        'state_rglru_h': nrm(ks[6], (DEPTH, DEC_BATCH, D_RNN), 0.5),
        'state_mlstm_conv': nrm(ks[7], (DEPTH, DEC_BATCH, CONV_W - 1, D_ML), 1.0),
        'state_mlstm_C': nrm(ks[8], (DEPTH, DEC_BATCH, H_ML, DV, DK), 0.1),
        'state_mlstm_n': nrm(ks[9], (DEPTH, DEC_BATCH, H_ML, DK), 0.1),
        'state_mlstm_m': nrm(ks[10], (DEPTH, DEC_BATCH, H_ML), 1.0),
        'norm_mix': 1.0 + nrm(ks[11], (DEPTH, D_MODEL), 0.02),
        'w_in': nrm(ks[12], (DEPTH, D_MODEL, D_IN_TOTAL), D_MODEL ** -0.5),
        'b_if': b_if,
        'conv_rglru_w': nrm(ks[13], (DEPTH, CONV_W, D_RNN), CONV_W ** -0.5),
        'conv_rglru_b': nrm(ks[14], (DEPTH, D_RNN), 0.02),
        'rglru_wa': nrm(ks[15], (DEPTH, RG_BLOCKS, RG_BW, RG_BW), RG_BW ** -0.5),
        'rglru_ba': nrm(ks[16], (DEPTH, D_RNN), 0.02),
        'rglru_wx': nrm(ks[17], (DEPTH, RG_BLOCKS, RG_BW, RG_BW), RG_BW ** -0.5),
        'rglru_bx': nrm(ks[18], (DEPTH, D_RNN), 0.02),
        'rglru_lambda': lam,
        'conv_mlstm_w': nrm(ks[19], (DEPTH, CONV_W, D_ML), CONV_W ** -0.5),
        'conv_mlstm_b': nrm(ks[20], (DEPTH, D_ML), 0.02),
        'mlstm_wq': nrm(ks[21], (DEPTH, H_ML, DH_IN, DK), DH_IN ** -0.5),
        'mlstm_wk': nrm(ks[22], (DEPTH, H_ML, DH_IN, DK), DH_IN ** -0.5),
        'mlstm_wv': nrm(ks[23], (DEPTH, H_ML, DH_IN, DV), DH_IN ** -0.5),
        'w_branch_a': nrm(ks[24], (DEPTH, D_RNN, D_MODEL), D_RNN ** -0.5),
        'w_branch_b': nrm(ks[25], (DEPTH, D_ML, D_MODEL), D_ML ** -0.5),
        'w_out': nrm(ks[26], (DEPTH, D_MODEL, D_MODEL), D_MODEL ** -0.5),
        'norm_mlp': 1.0 + nrm(ks[27], (DEPTH, D_MODEL), 0.02),
        'w_mlp_up': nrm(ks[28], (DEPTH, D_MODEL, D_FF), D_MODEL ** -0.5),
        'w_mlp_down': nrm(ks[29], (DEPTH, D_FF, D_MODEL), D_FF ** -0.5),
        'norm_final': 1.0 + nrm(ks[30], (D_MODEL,), 0.02),
    }


def reference(x_prompt, x_sample, state_rglru_conv, state_rglru_h, state_mlstm_conv, state_mlstm_C,
              state_mlstm_n, state_mlstm_m, norm_mix, w_in, b_if, conv_rglru_w, conv_rglru_b,
              rglru_wa, rglru_ba, rglru_wx, rglru_bx, rglru_lambda, conv_mlstm_w, conv_mlstm_b,
              mlstm_wq, mlstm_wk, mlstm_wv, w_branch_a, w_branch_b, w_out, norm_mlp,
              w_mlp_up, w_mlp_down, norm_final):
    Bp = x_prompt.shape[0]
    f32 = jnp.float32
    xp, xs = x_prompt, x_sample
    p_new = [[] for _ in range(6)]
    s_new = [[] for _ in range(6)]
    for l in range(DEPTH):
        w = (norm_mix[l], w_in[l], b_if[l], conv_rglru_w[l], conv_rglru_b[l], rglru_wa[l], rglru_ba[l],
             rglru_wx[l], rglru_bx[l], rglru_lambda[l], conv_mlstm_w[l], conv_mlstm_b[l], mlstm_wq[l],
             mlstm_wk[l], mlstm_wv[l], w_branch_a[l], w_branch_b[l], w_out[l], norm_mlp[l],
             w_mlp_up[l], w_mlp_down[l])
        outp = block(xp,
                     jnp.zeros((Bp, CONV_W - 1, D_RNN), xp.dtype), jnp.zeros((Bp, D_RNN), f32),
                     jnp.zeros((Bp, CONV_W - 1, D_ML), xp.dtype), jnp.zeros((Bp, H_ML, DV, DK), f32),
                     jnp.zeros((Bp, H_ML, DK), f32), jnp.zeros((Bp, H_ML), f32), *w)
        xp = outp[0]
        outs = block(xs, state_rglru_conv[l], state_rglru_h[l], state_mlstm_conv[l], state_mlstm_C[l],
                     state_mlstm_n[l], state_mlstm_m[l], *w)
        xs = outs[0]
        for j in range(6):
            p_new[j].append(outp[j + 1])
            s_new[j].append(outs[j + 1])
    y_prompt = rmsnorm(xp, norm_final)
    y_sample = rmsnorm(xs, norm_final)
    p_conv_r, p_h, p_conv_m, p_C, p_n, p_m = [jnp.stack(t, axis=0) for t in p_new]
    s_conv_r, s_h, s_conv_m, s_C, s_n, s_m = [jnp.stack(t, axis=0) for t in s_new]
    return (y_prompt, y_sample, p_conv_r, p_h, p_conv_m, p_C, p_n, p_m,
            s_conv_r, s_h, s_conv_m, s_C, s_n, s_m)
```

```python
import functools

import jax
import jax.numpy as jnp
from jax import lax
from jax.experimental import pallas as pl
from jax.experimental.pallas import tpu as pltpu

D_MODEL = 1024
CONV_W = 4
D_RNN = 1024
RG_BLOCKS = 8
RG_BW = D_RNN // RG_BLOCKS
RG_C = 8.0
D_ML = 2 * D_MODEL
H_ML = 8
DH_IN = D_ML // H_ML
DK = 128
DV = 256
D_FF = 4 * D_MODEL
EPS = 1e-6

SUBLANES = 8
VMEM_LIMIT_BYTES = 60 * 1024 * 1024

F32 = jnp.float32
BF16 = jnp.bfloat16


def _dot(a, b):
    return jnp.dot(a, b, preferred_element_type=F32)


def _dot_nt(a, b):
    return lax.dot_general(a, b, (((1,), (1,)), ((), ())), preferred_element_type=F32)


def _dot_tn(a, b):
    return lax.dot_general(a, b, (((0,), (0,)), ((), ())), preferred_element_type=F32)


def _rmsnorm(x, g):
    return x * lax.rsqrt(jnp.mean(x * x, axis=-1, keepdims=True) + EPS) * g


def _causal_conv(x, hist, w_ref, b_ref):
    t = x.shape[0]
    xe = jnp.concatenate([hist, x], axis=0)
    y = b_ref[...] + x * w_ref[CONV_W - 1:CONV_W, :]
    for k in range(1, CONV_W):
        y = y + pltpu.roll(xe, k, axis=0)[SUBLANES:] * w_ref[CONV_W - 1 - k:CONV_W - k, :]
    return y, xe[t:]


def _linear_scan(a, b, h0):
    t = a.shape[0]
    row = lax.broadcasted_iota(jnp.int32, a.shape, 0)
    k = 1
    while k < t:
        keep = row >= k
        b = jnp.where(keep, b + a * pltpu.roll(b, k, axis=0), b)
        a = jnp.where(keep, a * pltpu.roll(a, k, axis=0), a)
        k *= 2
    return b + a * h0


def _mixer_kernel(
        x_ref, convr0_ref, h0_ref, convm0_ref, c0_ref, n0_ref, m0_ref,
        gmix_ref, wxrgr_ref, wxm_ref, wzm_ref, wgab_ref,
        wicol_ref, wfcol_ref, wifrow_ref, bicol_ref, bfcol_ref, bifrow_ref,
        cwr_ref, cbr_ref, wax_ref, ba_ref, bx_ref, lam_ref,
        cwm_ref, cbm_ref, wqk_ref, wv_ref, wpa_ref, wpb_ref, wout_ref,
        y_ref, convr_ref, h_ref, convm_ref, c_ref, n_ref, m_ref,
        histr_s, histm_s, qk_s, v_s, hm_s, mix_s,
        *, nb, ts, chunk):
    j = pl.program_id(1)
    nj = pl.num_programs(1)
    rows = nb * ts

    @pl.when(j == 0)
    def _():
        histr_s[:, SUBLANES - (CONV_W - 1):, :] = convr0_ref[...]
        histm_s[:, SUBLANES - (CONV_W - 1):, :] = convm0_ref[...]
        histr_s[:, :SUBLANES - (CONV_W - 1), :] = jnp.zeros((nb, SUBLANES - (CONV_W - 1), D_RNN), F32)
        histm_s[:, :SUBLANES - (CONV_W - 1), :] = jnp.zeros((nb, SUBLANES - (CONV_W - 1), D_ML), F32)
        h_ref[...] = h0_ref[...]
        c_ref[...] = c0_ref[...]
        n_ref[...] = n0_ref[...]
        m_ref[...] = m0_ref[...]

    x = x_ref[...].reshape(rows, D_MODEL)
    u = _rmsnorm(x, gmix_ref[...]).astype(BF16)

    xrgr = _dot(u, wxrgr_ref[...])
    xr = xrgr[:, :D_RNN]
    gr = xrgr[:, D_RNN:]
    neg_c_softplus = -RG_C * jax.nn.softplus(-lam_ref[...])
    h_parts = []
    for b in range(nb):
        rs = slice(b * ts, (b + 1) * ts)
        xc, hist = _causal_conv(xr[rs], histr_s[b], cwr_ref, cbr_ref)
        histr_s[b] = hist
        xcb = xc.astype(BF16)
        r_parts, i_parts = [], []
        for blk in range(RG_BLOCKS):
            g = _dot(xcb[:, blk * RG_BW:(blk + 1) * RG_BW], wax_ref[blk])
            r_parts.append(g[:, :RG_BW])
            i_parts.append(g[:, RG_BW:])
        r = jax.nn.sigmoid(jnp.concatenate(r_parts, axis=1) + ba_ref[...])
        i = jax.nn.sigmoid(jnp.concatenate(i_parts, axis=1) + bx_ref[...])
        log_a = r * neg_c_softplus
        a = jnp.exp(log_a)
        gain = jnp.sqrt(-jnp.tanh(log_a) * (a * a + 1.0))
        hseq = _linear_scan(a, gain * (i * xc), h_ref[b])
        h_ref[b] = hseq[ts - 1:ts, :]
        h_parts.append(hseq)
    hseq = h_parts[0] if nb == 1 else jnp.concatenate(h_parts, axis=0)
    y_a = (hseq * jax.nn.gelu(gr)).astype(BF16)
    gab = _dot(u, wgab_ref[...])
    mix_s[...] = jax.nn.sigmoid(gab[:, :D_MODEL]) * _dot(y_a, wpa_ref[...])
    sig_gb = jax.nn.sigmoid(gab[:, D_MODEL:])

    xm = _dot(u, wxm_ref[...])
    xmb = xm.astype(BF16)
    for hd in range(H_ML):
        cs = slice(hd * DH_IN, (hd + 1) * DH_IN)
        v_s[:, cs] = _dot(xmb[:, cs], wv_ref[hd]).astype(BF16)

    n_chunks = ts // chunk
    tpos = lax.broadcasted_iota(jnp.int32, (ts, ts), 0)
    spos = lax.broadcasted_iota(jnp.int32, (ts, ts), 1)
    if n_chunks == 1:
        same_chunk_tri = spos <= tpos
    else:
        same_chunk_tri = (spos <= tpos) & ((tpos // chunk) == (spos // chunk))
    cum_mat = same_chunk_tri.astype(F32)
    tri = (lax.broadcasted_iota(jnp.int32, (chunk, chunk), 1)
           <= lax.broadcasted_iota(jnp.int32, (chunk, chunk), 0))
    pos_in_chunk = lax.broadcasted_iota(jnp.int32, (ts, H_ML), 0) % chunk

    for b in range(nb):
        rs = slice(b * ts, (b + 1) * ts)
        xmc, hist = _causal_conv(xm[rs], histm_s[b], cwm_ref, cbm_ref)
        histm_s[b] = hist
        xmc = jax.nn.silu(xmc).astype(BF16)
        for hd in range(H_ML):
            cs = slice(hd * DH_IN, (hd + 1) * DH_IN)
            qk = _dot(xmc[:, cs], wqk_ref[hd])
            qk_s[rs, hd * 2 * DK:hd * 2 * DK + DK] = (qk[:, :DK] * (DK ** -0.5)).astype(BF16)
            qk_s[rs, hd * 2 * DK + DK:(hd + 1) * 2 * DK] = qk[:, DK:].astype(BF16)

        ub = u[rs]
        ig_col = _dot(ub, wicol_ref[...]) + bicol_ref[...]
        lf_col = jax.nn.log_sigmoid(_dot(ub, wfcol_ref[...]) + bfcol_ref[...])
        g_row = _dot_nt(wifrow_ref[...], ub) + bifrow_ref[...]
        ig_row = g_row[:H_ML]
        lf_row = jax.nn.log_sigmoid(g_row[H_ML:])
        bcum_col = jnp.dot(cum_mat, lf_col, precision=lax.Precision.HIGHEST,
                           preferred_element_type=F32)
        bcum_row = lax.dot_general(lf_row, cum_mat, (((1,), (1,)), ((), ())),
                                   precision=lax.Precision.HIGHEST,
                                   preferred_element_type=F32)
        src_row = ig_row - bcum_row
        cmax = ig_col - bcum_col
        k = 1
        while k < chunk:
            cmax = jnp.where(pos_in_chunk >= k, jnp.maximum(cmax, pltpu.roll(cmax, k, axis=0)), cmax)
            k *= 2

        for c in range(n_chunks):
            ls = slice(c * chunk, (c + 1) * chunk)
            gs = slice(b * ts + c * chunk, b * ts + (c + 1) * chunk)
            m_prev = m_ref[b]
            bc = bcum_col[ls]
            inter = bc + m_prev
            m_t = jnp.maximum(inter, bc + cmax[ls])
            w_inter = jnp.exp(inter - m_t)
            floor = jnp.exp(-m_t)
            col_t = bc - m_t
            b_last = bc[chunk - 1:chunk, :]
            m_new = m_t[chunk - 1:chunk, :]
            wk_col = jnp.exp(b_last - bc + ig_col[ls] - m_new)
            decay = jnp.exp(b_last + m_prev - m_new)
            m_ref[b] = m_new
            for hd in range(H_ML):
                q = qk_s[gs, hd * 2 * DK:hd * 2 * DK + DK]
                kk = qk_s[gs, hd * 2 * DK + DK:(hd + 1) * 2 * DK]
                v = v_s[gs, hd * DV:(hd + 1) * DV]
                c_old = c_ref[b, hd]
                n_old = n_ref[b, hd:hd + 1, :]
                dmat = col_t[:, hd:hd + 1] + src_row[hd:hd + 1, ls]
                w_intra = jnp.exp(jnp.where(tri, dmat, -jnp.inf))
                s = _dot_nt(q, kk) * w_intra
                wi = w_inter[:, hd:hd + 1]
                num = _dot(s.astype(BF16), v) + wi * _dot_nt(q, c_old.astype(BF16))
                qn = jnp.sum(q.astype(F32) * n_old.astype(BF16).astype(F32), axis=1, keepdims=True)
                den = jnp.sum(s, axis=1, keepdims=True) + wi * qn
                hm_s[gs, hd * DV:(hd + 1) * DV] = num / jnp.maximum(jnp.abs(den), floor[:, hd:hd + 1])
                kw = kk.astype(F32) * wk_col[:, hd:hd + 1]
                dec = decay[:, hd:hd + 1]
                c_ref[b, hd] = dec * c_old + _dot_tn(v, kw.astype(BF16))
                n_ref[b, hd:hd + 1, :] = dec * n_old + jnp.sum(kw, axis=0, keepdims=True)

    y_b = (jax.nn.sigmoid(_dot(u, wzm_ref[...])) * hm_s[...]).astype(BF16)
    mix = mix_s[...] + sig_gb * _dot(y_b, wpb_ref[...])
    y_ref[...] = (x + _dot(mix.astype(BF16), wout_ref[...])).reshape(nb, ts, D_MODEL)

    @pl.when(j == nj - 1)
    def _():
        convr_ref[...] = histr_s[:, SUBLANES - (CONV_W - 1):, :]
        convm_ref[...] = histm_s[:, SUBLANES - (CONV_W - 1):, :]


def _mlp_kernel(x_ref, gmlp_ref, wup_ref, wdown_ref, gfin_ref, y_ref):
    x = x_ref[...]
    hid = jnp.maximum(_dot(_rmsnorm(x, gmlp_ref[...]).astype(BF16), wup_ref[...]), 0.0)
    x = x + _dot((hid * hid).astype(BF16), wdown_ref[...])
    y_ref[...] = _rmsnorm(x, gfin_ref[...])


def _resident(shape):
    zeros = (0,) * len(shape)
    return pl.BlockSpec(shape, lambda *_: zeros, pipeline_mode=pl.Buffered(1))


def _mixer(x, conv_r, h, conv_m, c, n, m, weights, *, nb, ts, chunk):
    batch, seq, _ = x.shape
    assert batch % nb == 0 and seq % ts == 0 and ts % chunk == 0 and chunk % SUBLANES == 0
    grid = (batch // nb, seq // ts)
    rows = nb * ts
    h = h.reshape(batch, 1, D_RNN)
    m = m.reshape(batch, 1, H_ML)

    def per_batch(shape, **kw):
        nd = len(shape)
        return pl.BlockSpec((nb,) + shape[1:], lambda i, j: (i,) + (0,) * (nd - 1), **kw)

    state_in = (conv_r, h, conv_m, c, n, m)
    in_specs = ([pl.BlockSpec((nb, ts, D_MODEL), lambda i, j: (i, j, 0))]
                + [per_batch(s.shape, pipeline_mode=pl.Buffered(1)) for s in state_in]
                + [_resident(w.shape) for w in weights])
    out_shape = ([jax.ShapeDtypeStruct(x.shape, F32)]
                 + [jax.ShapeDtypeStruct(s.shape, F32) for s in state_in])
    out_specs = ([pl.BlockSpec((nb, ts, D_MODEL), lambda i, j: (i, j, 0))]
                 + [per_batch(s.shape) for s in state_in])
    scratch = [
        pltpu.VMEM((nb, SUBLANES, D_RNN), F32),
        pltpu.VMEM((nb, SUBLANES, D_ML), F32),
        pltpu.VMEM((rows, H_ML * 2 * DK), BF16),
        pltpu.VMEM((rows, H_ML * DV), BF16),
        pltpu.VMEM((rows, H_ML * DV), F32),
        pltpu.VMEM((rows, D_MODEL), F32),
    ]
    outs = pl.pallas_call(
        functools.partial(_mixer_kernel, nb=nb, ts=ts, chunk=chunk),
        grid=grid,
        in_specs=in_specs,
        out_specs=out_specs,
        out_shape=out_shape,
        scratch_shapes=scratch,
        compiler_params=pltpu.CompilerParams(
            dimension_semantics=("arbitrary", "arbitrary"),
            vmem_limit_bytes=VMEM_LIMIT_BYTES),
    )(x, *state_in, *weights)
    y, conv_r, h, conv_m, c, n, m = outs
    return y, conv_r, h.reshape(batch, D_RNN), conv_m, c, n, m.reshape(batch, H_ML)


def _mlp(x, weights, *, rows):
    batch, seq, _ = x.shape
    total = batch * seq
    assert total % rows == 0
    x2 = x.reshape(total, D_MODEL)
    y = pl.pallas_call(
        _mlp_kernel,
        grid=(total // rows,),
        in_specs=[pl.BlockSpec((rows, D_MODEL), lambda i: (i, 0))]
                 + [_resident(w.shape) for w in weights],
        out_specs=pl.BlockSpec((rows, D_MODEL), lambda i: (i, 0)),
        out_shape=jax.ShapeDtypeStruct((total, D_MODEL), F32),
        compiler_params=pltpu.CompilerParams(
            dimension_semantics=("arbitrary",),
            vmem_limit_bytes=VMEM_LIMIT_BYTES),
    )(x2, *weights)
    return y.reshape(batch, seq, D_MODEL)


def kernel(x_prompt, x_sample, state_rglru_conv, state_rglru_h, state_mlstm_conv, state_mlstm_C, state_mlstm_n, state_mlstm_m, norm_mix, w_in, b_if, conv_rglru_w, conv_rglru_b, rglru_wa, rglru_ba, rglru_wx, rglru_bx, rglru_lambda, conv_mlstm_w, conv_mlstm_b, mlstm_wq, mlstm_wk, mlstm_wv, w_branch_a, w_branch_b, w_out, norm_mlp, w_mlp_up, w_mlp_down, norm_final):
    depth = w_in.shape[0]
    bp = x_prompt.shape[0]
    xp, xs = x_prompt, x_sample
    p_new = [[] for _ in range(6)]
    s_new = [[] for _ in range(6)]
    row = lambda t: t.reshape(1, -1)
    for l in range(depth):
        wl = w_in[l]
        o_xm = 2 * D_RNN
        o_zm = o_xm + D_ML
        o_i = o_zm + D_ML
        o_f = o_i + H_ML
        o_g = o_f + H_ML
        mixer_w = (
            row(norm_mix[l]),
            wl[:, :o_xm].astype(BF16),
            wl[:, o_xm:o_zm].astype(BF16),
            wl[:, o_zm:o_i].astype(BF16),
            wl[:, o_g:].astype(BF16),
            wl[:, o_i:o_f].astype(BF16),
            wl[:, o_f:o_g].astype(BF16),
            wl[:, o_i:o_g].T.astype(BF16),
            row(b_if[l, :H_ML]),
            row(b_if[l, H_ML:]),
            b_if[l].reshape(2 * H_ML, 1),
            conv_rglru_w[l], row(conv_rglru_b[l]),
            jnp.concatenate([rglru_wa[l], rglru_wx[l]], axis=-1).astype(BF16),
            row(rglru_ba[l]), row(rglru_bx[l]), row(rglru_lambda[l]),
            conv_mlstm_w[l], row(conv_mlstm_b[l]),
            jnp.concatenate([mlstm_wq[l], mlstm_wk[l]], axis=-1).astype(BF16),
            mlstm_wv[l].astype(BF16),
            w_branch_a[l].astype(BF16), w_branch_b[l].astype(BF16), w_out[l].astype(BF16),
        )
        outp = _mixer(xp,
                      jnp.zeros((bp, CONV_W - 1, D_RNN), F32), jnp.zeros((bp, D_RNN), F32),
                      jnp.zeros((bp, CONV_W - 1, D_ML), F32), jnp.zeros((bp, H_ML, DV, DK), F32),
                      jnp.zeros((bp, H_ML, DK), F32), jnp.zeros((bp, H_ML), F32),
                      mixer_w, nb=1, ts=256, chunk=128)
        outs = _mixer(xs, state_rglru_conv[l], state_rglru_h[l], state_mlstm_conv[l],
                      state_mlstm_C[l], state_mlstm_n[l], state_mlstm_m[l],
                      mixer_w, nb=4, ts=64, chunk=64)
        xp, xs = outp[0], outs[0]
        for jj in range(6):
            p_new[jj].append(outp[jj + 1])
            s_new[jj].append(outs[jj + 1])
        last = l == depth - 1
        mlp_w = (row(norm_mlp[l]), w_mlp_up[l].astype(BF16), w_mlp_down[l].astype(BF16),
                 row(norm_final) if last else None)
        assert last, "final norm is fused into the last layer's MLP kernel"
        xp = _mlp(xp, mlp_w, rows=512)
        xs = _mlp(xs, mlp_w, rows=512)
    p_state = [jnp.stack(t, axis=0) for t in p_new]
    s_state = [jnp.stack(t, axis=0) for t in s_new]
    return (xp, xs, *p_state, *s_state)
```

```python
import functools

import jax
import jax.numpy as jnp
from jax import lax
from jax.experimental import pallas as pl
from jax.experimental.pallas import tpu as pltpu

D_MODEL = 1024
CONV_W = 4
D_RNN = 1024
RG_BLOCKS = 8
RG_BW = D_RNN // RG_BLOCKS
RG_C = 8.0
D_ML = 2 * D_MODEL
H_ML = 8
DH_IN = D_ML // H_ML
DK = 128
DV = 256
D_FF = 4 * D_MODEL
EPS = 1e-6

LANES = 128
SUBLANES = 8
ROW_STRIDE = 4
SPAN = SUBLANES * ROW_STRIDE
VMEM_LIMIT_BYTES = 60 * 1024 * 1024

F32 = jnp.float32
BF16 = jnp.bfloat16


def _dot(a, b):
    return jnp.dot(a, b, preferred_element_type=F32)


def _dot_nt(a, b):
    return lax.dot_general(a, b, (((1,), (1,)), ((), ())), preferred_element_type=F32)


def _dot_tn(a, b):
    return lax.dot_general(a, b, (((0,), (0,)), ((), ())), preferred_element_type=F32)


def _wb(w_ref, *idx):
    return pltpu.bitcast(w_ref[idx] if idx else w_ref[...], BF16)


def _pack_rows(w):
    wb = w.astype(BF16)
    *lead, k, n = wb.shape
    wb = jnp.swapaxes(wb.reshape(*lead, k // 2, 2, n), -1, -2)
    return lax.bitcast_convert_type(wb, jnp.uint32)


def _rmsnorm(x, g):
    return x * lax.rsqrt(jnp.mean(x * x, axis=-1, keepdims=True) + EPS) * g


def _strided(start):
    return pl.ds(start, SUBLANES, stride=ROW_STRIDE)


def _causal_conv(xe_ref, out_ref, b, c, t, w_ref, b_ref, act=None):
    cs = slice(c * LANES, (c + 1) * LANES)
    taps = [jnp.broadcast_to(w_ref[k:k + 1, cs], (SUBLANES, LANES)) for k in range(CONV_W)]
    bias = jnp.broadcast_to(b_ref[:, cs], (SUBLANES, LANES))
    for start in range(0, t, SPAN):
        for r in range(ROW_STRIDE):
            y = bias
            for k in range(CONV_W):
                y = y + xe_ref[b, c, _strided(SUBLANES + start + r - k), :] * taps[CONV_W - 1 - k]
            out_ref[b, c, _strided(start + r), :] = y if act is None else act(y)


def _rglru_scan(xc_ref, gate_ref, b, c, t, rate, h0):
    row = lax.broadcasted_iota(jnp.int32, (SUBLANES, LANES), 0)
    rate = jnp.broadcast_to(rate, (SUBLANES, LANES))
    carry = jnp.broadcast_to(h0, (SUBLANES, LANES))
    for start in range(0, t, SPAN):
        h_loc, a_cum = [], []
        for r in range(ROW_STRIDE):
            rows = _strided(start + r)
            xc = xc_ref[b, c, rows, :]
            log_a = jax.nn.sigmoid(gate_ref[b, c, 0, rows, :]) * rate
            a = jnp.exp(log_a)
            gain = jnp.sqrt(-jnp.tanh(log_a) * (a * a + 1.0))
            bt = gain * (jax.nn.sigmoid(gate_ref[b, c, 1, rows, :]) * xc)
            if r == 0:
                h_loc.append(bt)
                a_cum.append(a)
            else:
                h_loc.append(a * h_loc[-1] + bt)
                a_cum.append(a * a_cum[-1])
        h_end, a_end = h_loc[-1], a_cum[-1]
        k = 1
        while k < SUBLANES:
            keep = row >= k
            h_end = h_end + a_end * jnp.where(keep, pltpu.roll(h_end, k, axis=0), 0.0)
            a_end = a_end * jnp.where(keep, pltpu.roll(a_end, k, axis=0), 1.0)
            k *= 2
        h_end = h_end + a_end * carry
        h_in = jnp.where(row >= 1, pltpu.roll(h_end, 1, axis=0), carry)
        for r in range(ROW_STRIDE):
            xc_ref[b, c, _strided(start + r), :] = h_loc[r] + a_cum[r] * h_in
        carry = jnp.broadcast_to(h_end[SUBLANES - 1:SUBLANES, :], (SUBLANES, LANES))
    return carry[0:1, :]


def _mixer_kernel(
        x_ref, convr0_ref, h0_ref, convm0_ref, c0_ref, n0_ref, m0_ref,
        gmix_ref, wxrgr_ref, wxm_ref, wzm_ref, wgab_ref,
        wicol_ref, wfcol_ref, wifrow_ref, bicol_ref, bfcol_ref, bifrow_ref,
        cwr_ref, cbr_ref, wax_ref, ba_ref, bx_ref, lam_ref,
        cwm_ref, cbm_ref, wqk_ref, wv_ref, wpa_ref, wpb_ref, wout_ref,
        y_ref, convr_ref, h_ref, convm_ref, c_ref, n_ref, m_ref,
        xer_s, xem_s, xc_s, gate_s, xmc_s, ya_s, qk_s, v_s, hm_s,
        *, nb, ts, chunk):
    j = pl.program_id(1)
    nj = pl.num_programs(1)
    rows = nb * ts
    hist = slice(SUBLANES - (CONV_W - 1), SUBLANES)
    tail = slice(SUBLANES + ts - (CONV_W - 1), SUBLANES + ts)
    lane_blk = lambda c: slice(c * LANES, (c + 1) * LANES)

    @pl.when(j == 0)
    def _():
        for c in range(D_RNN // LANES):
            xer_s[:, c, hist, :] = convr0_ref[:, :, lane_blk(c)]
        for c in range(D_ML // LANES):
            xem_s[:, c, hist, :] = convm0_ref[:, :, lane_blk(c)]
        h_ref[...] = h0_ref[...]
        c_ref[...] = c0_ref[...]
        n_ref[...] = n0_ref[...]
        m_ref[...] = m0_ref[...]

    x = x_ref[...].reshape(rows, D_MODEL)
    u = _rmsnorm(x, gmix_ref[...]).astype(BF16)

    xrgr = _dot(u, _wb(wxrgr_ref))
    gr = xrgr[:, D_RNN:]
    rate = -RG_C * jax.nn.softplus(-lam_ref[...])
    for b in range(nb):
        rs = slice(b * ts, (b + 1) * ts)
        for c in range(RG_BLOCKS):
            cs = lane_blk(c)
            xer_s[b, c, SUBLANES:, :] = xrgr[rs, cs]
            _causal_conv(xer_s, xc_s, b, c, ts, cwr_ref, cbr_ref)
            g = _dot(xc_s[b, c].astype(BF16), _wb(wax_ref, c))
            gate_s[b, c, 0] = g[:, :RG_BW] + ba_ref[:, cs]
            gate_s[b, c, 1] = g[:, RG_BW:] + bx_ref[:, cs]
            h_ref[b, :, cs] = _rglru_scan(xc_s, gate_s, b, c, ts, rate[:, cs], h_ref[b, :, cs])
            ya_s[rs, cs] = (xc_s[b, c] * jax.nn.gelu(gr[rs, cs])).astype(BF16)

    xm = _dot(u, _wb(wxm_ref))
    xmb = xm.astype(BF16)
    for hd in range(H_ML):
        cs = slice(hd * DH_IN, (hd + 1) * DH_IN)
        v_s[:, cs] = _dot(xmb[:, cs], _wb(wv_ref, hd)).astype(BF16)

    n_chunks = ts // chunk
    tpos = lax.broadcasted_iota(jnp.int32, (ts, ts), 0)
    spos = lax.broadcasted_iota(jnp.int32, (ts, ts), 1)
    if n_chunks == 1:
        same_chunk_tri = spos <= tpos
    else:
        same_chunk_tri = (spos <= tpos) & ((tpos // chunk) == (spos // chunk))
    cum_mat = same_chunk_tri.astype(F32)
    tri = (lax.broadcasted_iota(jnp.int32, (chunk, chunk), 1)
           <= lax.broadcasted_iota(jnp.int32, (chunk, chunk), 0))
    pos_in_chunk = lax.broadcasted_iota(jnp.int32, (ts, H_ML), 0) % chunk

    for b in range(nb):
        rs = slice(b * ts, (b + 1) * ts)
        for c in range(D_ML // LANES):
            xem_s[b, c, SUBLANES:, :] = xm[rs, lane_blk(c)]
            _causal_conv(xem_s, xmc_s, b, c, ts, cwm_ref, cbm_ref, act=jax.nn.silu)
        blocks_per_head = DH_IN // LANES
        for hd in range(H_ML):
            xmc = jnp.concatenate([xmc_s[b, hd * blocks_per_head + i] for i in range(blocks_per_head)],
                                  axis=1).astype(BF16)
            qk = _dot(xmc, _wb(wqk_ref, hd))
            qk_s[rs, hd * 2 * DK:hd * 2 * DK + DK] = (qk[:, :DK] * (DK ** -0.5)).astype(BF16)
            qk_s[rs, hd * 2 * DK + DK:(hd + 1) * 2 * DK] = qk[:, DK:].astype(BF16)

        ub = u[rs]
        ig_col = _dot(ub, wicol_ref[...]) + bicol_ref[...]
        lf_col = jax.nn.log_sigmoid(_dot(ub, wfcol_ref[...]) + bfcol_ref[...])
        g_row = _dot_nt(wifrow_ref[...], ub) + bifrow_ref[...]
        ig_row = g_row[:H_ML]
        lf_row = jax.nn.log_sigmoid(g_row[H_ML:])
        bcum_col = jnp.dot(cum_mat, lf_col, precision=lax.Precision.HIGHEST,
                           preferred_element_type=F32)
        bcum_row = lax.dot_general(lf_row, cum_mat, (((1,), (1,)), ((), ())),
                                   precision=lax.Precision.HIGHEST,
                                   preferred_element_type=F32)
        src_row = ig_row - bcum_row
        cmax = ig_col - bcum_col
        k = 1
        while k < chunk:
            cmax = jnp.where(pos_in_chunk >= k, jnp.maximum(cmax, pltpu.roll(cmax, k, axis=0)), cmax)
            k *= 2

        for c in range(n_chunks):
            ls = slice(c * chunk, (c + 1) * chunk)
            gs = slice(b * ts + c * chunk, b * ts + (c + 1) * chunk)
            m_prev = m_ref[b]
            bc = bcum_col[ls]
            inter = bc + m_prev
            m_t = jnp.maximum(inter, bc + cmax[ls])
            w_inter = jnp.exp(inter - m_t)
            floor = jnp.exp(-m_t)
            col_t = bc - m_t
            b_last = bc[chunk - 1:chunk, :]
            m_new = m_t[chunk - 1:chunk, :]
            wk_col = jnp.exp(b_last - bc + ig_col[ls] - m_new)
            decay = jnp.exp(b_last + m_prev - m_new)
            m_ref[b] = m_new
            for hd in range(H_ML):
                q = qk_s[gs, hd * 2 * DK:hd * 2 * DK + DK]
                kk = qk_s[gs, hd * 2 * DK + DK:(hd + 1) * 2 * DK]
                v = v_s[gs, hd * DV:(hd + 1) * DV]
                c_old = c_ref[b, hd]
                n_old = n_ref[b, hd:hd + 1, :]
                dmat = col_t[:, hd:hd + 1] + src_row[hd:hd + 1, ls]
                w_intra = jnp.exp(jnp.where(tri, dmat, -jnp.inf))
                s = _dot_nt(q, kk) * w_intra
                wi = w_inter[:, hd:hd + 1]
                num = _dot(s.astype(BF16), v) + wi * _dot_nt(q, c_old.astype(BF16))
                qn = jnp.sum(q.astype(F32) * n_old.astype(BF16).astype(F32), axis=1, keepdims=True)
                den = jnp.sum(s, axis=1, keepdims=True) + wi * qn
                hm_s[gs, hd * DV:(hd + 1) * DV] = num / jnp.maximum(jnp.abs(den), floor[:, hd:hd + 1])
                kw = kk.astype(F32) * wk_col[:, hd:hd + 1]
                dec = decay[:, hd:hd + 1]
                c_ref[b, hd] = dec * c_old + _dot_tn(v, kw.astype(BF16))
                n_ref[b, hd:hd + 1, :] = dec * n_old + jnp.sum(kw, axis=0, keepdims=True)

    y_b = (jax.nn.sigmoid(_dot(u, _wb(wzm_ref))) * hm_s[...]).astype(BF16)
    gab = _dot(u, _wb(wgab_ref))
    mix = (jax.nn.sigmoid(gab[:, :D_MODEL]) * _dot(ya_s[...], _wb(wpa_ref))
           + jax.nn.sigmoid(gab[:, D_MODEL:]) * _dot(y_b, _wb(wpb_ref)))
    y_ref[...] = (x + _dot(mix.astype(BF16), _wb(wout_ref))).reshape(nb, ts, D_MODEL)

    @pl.when(j == nj - 1)
    def _():
        for c in range(D_RNN // LANES):
            convr_ref[:, :, lane_blk(c)] = xer_s[:, c, tail, :]
        for c in range(D_ML // LANES):
            convm_ref[:, :, lane_blk(c)] = xem_s[:, c, tail, :]

    xer_s[:, :, hist, :] = xer_s[:, :, tail, :]
    xem_s[:, :, hist, :] = xem_s[:, :, tail, :]


def _mlp_kernel(x_ref, gmlp_ref, wup_ref, wdown_ref, gfin_ref, y_ref):
    x = x_ref[...]
    hid = jnp.maximum(_dot(_rmsnorm(x, gmlp_ref[...]).astype(BF16), _wb(wup_ref)), 0.0)
    x = x + _dot((hid * hid).astype(BF16), _wb(wdown_ref))
    y_ref[...] = _rmsnorm(x, gfin_ref[...])


def _resident(shape):
    zeros = (0,) * len(shape)
    return pl.BlockSpec(shape, lambda *_: zeros, pipeline_mode=pl.Buffered(1))


def _mixer(x, conv_r, h, conv_m, c, n, m, weights, *, nb, ts, chunk):
    batch, seq, _ = x.shape
    assert batch % nb == 0 and seq % ts == 0 and ts % chunk == 0 and chunk % SUBLANES == 0
    assert ts % SPAN == 0
    grid = (batch // nb, seq // ts)
    rows = nb * ts
    h = h.reshape(batch, 1, D_RNN)
    m = m.reshape(batch, 1, H_ML)

    def per_batch(shape, **kw):
        nd = len(shape)
        return pl.BlockSpec((nb,) + shape[1:], lambda i, j: (i,) + (0,) * (nd - 1), **kw)

    state_in = (conv_r, h, conv_m, c, n, m)
    in_specs = ([pl.BlockSpec((nb, ts, D_MODEL), lambda i, j: (i, j, 0))]
                + [per_batch(s.shape, pipeline_mode=pl.Buffered(1)) for s in state_in]
                + [_resident(w.shape) for w in weights])
    out_shape = ([jax.ShapeDtypeStruct(x.shape, F32)]
                 + [jax.ShapeDtypeStruct(s.shape, F32) for s in state_in])
    out_specs = ([pl.BlockSpec((nb, ts, D_MODEL), lambda i, j: (i, j, 0))]
                 + [per_batch(s.shape, pipeline_mode=pl.Buffered(1)) for s in state_in])
    scratch = [
        pltpu.VMEM((nb, D_RNN // LANES, SUBLANES + ts, LANES), F32),
        pltpu.VMEM((nb, D_ML // LANES, SUBLANES + ts, LANES), F32),
        pltpu.VMEM((nb, D_RNN // LANES, ts, LANES), F32),
        pltpu.VMEM((nb, RG_BLOCKS, 2, ts, LANES), F32),
        pltpu.VMEM((nb, D_ML // LANES, ts, LANES), F32),
        pltpu.VMEM((rows, D_RNN), BF16),
        pltpu.VMEM((rows, H_ML * 2 * DK), BF16),
        pltpu.VMEM((rows, H_ML * DV), BF16),
        pltpu.VMEM((rows, H_ML * DV), F32),
    ]
    outs = pl.pallas_call(
        functools.partial(_mixer_kernel, nb=nb, ts=ts, chunk=chunk),
        grid=grid,
        in_specs=in_specs,
        out_specs=out_specs,
        out_shape=out_shape,
        scratch_shapes=scratch,
        compiler_params=pltpu.CompilerParams(
            dimension_semantics=("arbitrary", "arbitrary"),
            vmem_limit_bytes=VMEM_LIMIT_BYTES),
    )(x, *state_in, *weights)
    y, conv_r, h, conv_m, c, n, m = outs
    return y, conv_r, h.reshape(batch, D_RNN), conv_m, c, n, m.reshape(batch, H_ML)


def _mlp(x, weights, *, rows):
    batch, seq, _ = x.shape
    total = batch * seq
    assert total % rows == 0
    x2 = x.reshape(total, D_MODEL)
    y = pl.pallas_call(
        _mlp_kernel,
        grid=(total // rows,),
        in_specs=[pl.BlockSpec((rows, D_MODEL), lambda i: (i, 0))]
                 + [_resident(w.shape) for w in weights],
        out_specs=pl.BlockSpec((rows, D_MODEL), lambda i: (i, 0)),
        out_shape=jax.ShapeDtypeStruct((total, D_MODEL), F32),
        compiler_params=pltpu.CompilerParams(
            dimension_semantics=("arbitrary",),
            vmem_limit_bytes=VMEM_LIMIT_BYTES),
    )(x2, *weights)
    return y.reshape(batch, seq, D_MODEL)


def kernel(x_prompt, x_sample, state_rglru_conv, state_rglru_h, state_mlstm_conv, state_mlstm_C, state_mlstm_n, state_mlstm_m, norm_mix, w_in, b_if, conv_rglru_w, conv_rglru_b, rglru_wa, rglru_ba, rglru_wx, rglru_bx, rglru_lambda, conv_mlstm_w, conv_mlstm_b, mlstm_wq, mlstm_wk, mlstm_wv, w_branch_a, w_branch_b, w_out, norm_mlp, w_mlp_up, w_mlp_down, norm_final):
    depth = w_in.shape[0]
    assert depth == 1, "the final norm is fused into the (single) layer's MLP kernel"
    l = 0
    bp = x_prompt.shape[0]
    row = lambda t: t.reshape(1, -1)
    wl = w_in[l]
    o_xm = 2 * D_RNN
    o_zm = o_xm + D_ML
    o_i = o_zm + D_ML
    o_f = o_i + H_ML
    o_g = o_f + H_ML
    mixer_w = (
        row(norm_mix[l]),
        _pack_rows(wl[:, :o_xm]),
        _pack_rows(wl[:, o_xm:o_zm]),
        _pack_rows(wl[:, o_zm:o_i]),
        _pack_rows(wl[:, o_g:]),
        wl[:, o_i:o_f].astype(BF16),
        wl[:, o_f:o_g].astype(BF16),
        wl[:, o_i:o_g].T.astype(BF16),
        row(b_if[l, :H_ML]),
        row(b_if[l, H_ML:]),
        b_if[l].reshape(2 * H_ML, 1),
        conv_rglru_w[l], row(conv_rglru_b[l]),
        _pack_rows(jnp.concatenate([rglru_wa[l], rglru_wx[l]], axis=-1)),
        row(rglru_ba[l]), row(rglru_bx[l]), row(rglru_lambda[l]),
        conv_mlstm_w[l], row(conv_mlstm_b[l]),
        _pack_rows(jnp.concatenate([mlstm_wq[l], mlstm_wk[l]], axis=-1)),
        _pack_rows(mlstm_wv[l]),
        _pack_rows(w_branch_a[l]), _pack_rows(w_branch_b[l]), _pack_rows(w_out[l]),
    )
    outp = _mixer(x_prompt,
                  jnp.zeros((bp, CONV_W - 1, D_RNN), F32), jnp.zeros((bp, D_RNN), F32),
                  jnp.zeros((bp, CONV_W - 1, D_ML), F32), jnp.zeros((bp, H_ML, DV, DK), F32),
                  jnp.zeros((bp, H_ML, DK), F32), jnp.zeros((bp, H_ML), F32),
                  mixer_w, nb=1, ts=256, chunk=128)
    outs = _mixer(x_sample, state_rglru_conv[l], state_rglru_h[l], state_mlstm_conv[l],
                  state_mlstm_C[l], state_mlstm_n[l], state_mlstm_m[l],
                  mixer_w, nb=4, ts=64, chunk=64)
    mlp_w = (row(norm_mlp[l]), _pack_rows(w_mlp_up[l]), _pack_rows(w_mlp_down[l]), row(norm_final))
    y_prompt = _mlp(outp[0], mlp_w, rows=512)
    y_sample = _mlp(outs[0], mlp_w, rows=512)
    p_state = [t[None] for t in outp[1:]]
    s_state = [t[None] for t in outs[1:]]
    return (y_prompt, y_sample, *p_state, *s_state)
```

```python
import functools

import jax
import jax.numpy as jnp
from jax import lax
from jax.experimental import pallas as pl
from jax.experimental.pallas import tpu as pltpu

D_MODEL = 1024
CONV_W = 4
D_RNN = 1024
RG_BLOCKS = 8
RG_BW = D_RNN // RG_BLOCKS
RG_C = 8.0
D_ML = 2 * D_MODEL
H_ML = 8
DH_IN = D_ML // H_ML
DK = 128
DV = 256
D_FF = 4 * D_MODEL
EPS = 1e-6

LANES = 128
SUBLANES = 8
ROW_STRIDE = 4
SPAN = SUBLANES * ROW_STRIDE
VMEM_LIMIT_BYTES = 60 * 1024 * 1024
PACK_ROWS = 256

F32 = jnp.float32
BF16 = jnp.bfloat16


def _dot(a, b):
    return jnp.dot(a, b, preferred_element_type=F32)


def _dot_nt(a, b):
    return lax.dot_general(a, b, (((1,), (1,)), ((), ())), preferred_element_type=F32)


def _dot_tn(a, b):
    return lax.dot_general(a, b, (((0,), (0,)), ((), ())), preferred_element_type=F32)


def _wb(w_ref, *idx):
    return pltpu.bitcast(w_ref[idx] if idx else w_ref[...], BF16)


def _pack_kernel(w_ref, o_ref):
    o_ref[...] = pltpu.bitcast(w_ref[...].astype(BF16), jnp.uint32)


def _pack_rows(w):
    *lead, k, n = w.shape
    w2 = w.reshape(-1, n)
    total = w2.shape[0]
    rb = min(total, PACK_ROWS)
    assert k % 2 == 0 and total % rb == 0
    packed = pl.pallas_call(
        _pack_kernel,
        grid=(total // rb,),
        in_specs=[pl.BlockSpec((rb, n), lambda i: (i, 0))],
        out_specs=pl.BlockSpec((rb // 2, n), lambda i: (i, 0)),
        out_shape=jax.ShapeDtypeStruct((total // 2, n), jnp.uint32),
    )(w2)
    return packed.reshape(*lead, k // 2, n)


def _rmsnorm(x, g):
    return x * lax.rsqrt(jnp.mean(x * x, axis=-1, keepdims=True) + EPS) * g


def _strided(start):
    return pl.ds(start, SUBLANES, stride=ROW_STRIDE)


def _causal_conv(xe_ref, out_ref, b, c, t, w_ref, b_ref, act=None):
    cs = slice(c * LANES, (c + 1) * LANES)
    taps = [jnp.broadcast_to(w_ref[k:k + 1, cs], (SUBLANES, LANES)) for k in range(CONV_W)]
    bias = jnp.broadcast_to(b_ref[:, cs], (SUBLANES, LANES))
    for start in range(0, t, SPAN):
        for r in range(ROW_STRIDE):
            y = bias
            for k in range(CONV_W):
                y = y + xe_ref[b, c, _strided(SUBLANES + start + r - k), :] * taps[CONV_W - 1 - k]
            out_ref[b, c, _strided(start + r), :] = y if act is None else act(y)


def _rglru_scan(xc_ref, gate_ref, b, c, t, rate, h0):
    row = lax.broadcasted_iota(jnp.int32, (SUBLANES, LANES), 0)
    rate = jnp.broadcast_to(rate, (SUBLANES, LANES))
    carry = jnp.broadcast_to(h0, (SUBLANES, LANES))
    for start in range(0, t, SPAN):
        h_loc, a_cum = [], []
        for r in range(ROW_STRIDE):
            rows = _strided(start + r)
            xc = xc_ref[b, c, rows, :]
            log_a = jax.nn.sigmoid(gate_ref[b, c, 0, rows, :]) * rate
            a = jnp.exp(log_a)
            gain = jnp.sqrt(-jnp.tanh(log_a) * (a * a + 1.0))
            bt = gain * (jax.nn.sigmoid(gate_ref[b, c, 1, rows, :]) * xc)
            if r == 0:
                h_loc.append(bt)
                a_cum.append(a)
            else:
                h_loc.append(a * h_loc[-1] + bt)
                a_cum.append(a * a_cum[-1])
        h_end, a_end = h_loc[-1], a_cum[-1]
        k = 1
        while k < SUBLANES:
            keep = row >= k
            h_end = h_end + a_end * jnp.where(keep, pltpu.roll(h_end, k, axis=0), 0.0)
            a_end = a_end * jnp.where(keep, pltpu.roll(a_end, k, axis=0), 1.0)
            k *= 2
        h_end = h_end + a_end * carry
        h_in = jnp.where(row >= 1, pltpu.roll(h_end, 1, axis=0), carry)
        for r in range(ROW_STRIDE):
            xc_ref[b, c, _strided(start + r), :] = h_loc[r] + a_cum[r] * h_in
        carry = jnp.broadcast_to(h_end[SUBLANES - 1:SUBLANES, :], (SUBLANES, LANES))
    return carry[0:1, :]


def _mixer_kernel(
        x_ref, convr0_ref, h0_ref, convm0_ref, c0_ref, n0_ref, m0_ref,
        gmix_ref, wxrgr_ref, wxm_ref, wzm_ref, wgab_ref,
        wicol_ref, wfcol_ref, wifrow_ref, bicol_ref, bfcol_ref, bifrow_ref,
        cwr_ref, cbr_ref, wax_ref, ba_ref, bx_ref, lam_ref,
        cwm_ref, cbm_ref, wqk_ref, wv_ref, wpa_ref, wpb_ref, wout_ref,
        y_ref, convr_ref, h_ref, convm_ref, c_ref, n_ref, m_ref,
        xer_s, xem_s, xc_s, gate_s, xmc_s, ya_s, qk_s, v_s, hm_s,
        *, nb, ts, chunk):
    j = pl.program_id(1)
    nj = pl.num_programs(1)
    rows = nb * ts
    hist = slice(SUBLANES - (CONV_W - 1), SUBLANES)
    tail = slice(SUBLANES + ts - (CONV_W - 1), SUBLANES + ts)
    lane_blk = lambda c: slice(c * LANES, (c + 1) * LANES)

    @pl.when(j == 0)
    def _():
        for c in range(D_RNN // LANES):
            xer_s[:, c, hist, :] = convr0_ref[:, :, lane_blk(c)]
        for c in range(D_ML // LANES):
            xem_s[:, c, hist, :] = convm0_ref[:, :, lane_blk(c)]
        h_ref[...] = h0_ref[...]
        c_ref[...] = c0_ref[...]
        n_ref[...] = n0_ref[...]
        m_ref[...] = m0_ref[...]

    x = x_ref[...].reshape(rows, D_MODEL)
    u = _rmsnorm(x, gmix_ref[...]).astype(BF16)

    xrgr = _dot(u, _wb(wxrgr_ref))
    gr = xrgr[:, D_RNN:]
    rate = -RG_C * jax.nn.softplus(-lam_ref[...])
    for b in range(nb):
        rs = slice(b * ts, (b + 1) * ts)
        for c in range(RG_BLOCKS):
            cs = lane_blk(c)
            xer_s[b, c, SUBLANES:, :] = xrgr[rs, cs]
            _causal_conv(xer_s, xc_s, b, c, ts, cwr_ref, cbr_ref)
            g = _dot(xc_s[b, c].astype(BF16), _wb(wax_ref, c))
            gate_s[b, c, 0] = g[:, :RG_BW] + ba_ref[:, cs]
            gate_s[b, c, 1] = g[:, RG_BW:] + bx_ref[:, cs]
            h_ref[b, :, cs] = _rglru_scan(xc_s, gate_s, b, c, ts, rate[:, cs], h_ref[b, :, cs])
            ya_s[rs, cs] = (xc_s[b, c] * jax.nn.gelu(gr[rs, cs])).astype(BF16)

    xm = _dot(u, _wb(wxm_ref))
    xmb = xm.astype(BF16)
    for hd in range(H_ML):
        cs = slice(hd * DH_IN, (hd + 1) * DH_IN)
        v_s[:, cs] = _dot(xmb[:, cs], _wb(wv_ref, hd)).astype(BF16)

    n_chunks = ts // chunk
    tpos = lax.broadcasted_iota(jnp.int32, (ts, ts), 0)
    spos = lax.broadcasted_iota(jnp.int32, (ts, ts), 1)
    if n_chunks == 1:
        same_chunk_tri = spos <= tpos
    else:
        same_chunk_tri = (spos <= tpos) & ((tpos // chunk) == (spos // chunk))
    cum_mat = same_chunk_tri.astype(F32)
    tri = (lax.broadcasted_iota(jnp.int32, (chunk, chunk), 1)
           <= lax.broadcasted_iota(jnp.int32, (chunk, chunk), 0))
    pos_in_chunk = lax.broadcasted_iota(jnp.int32, (ts, H_ML), 0) % chunk

    for b in range(nb):
        rs = slice(b * ts, (b + 1) * ts)
        for c in range(D_ML // LANES):
            xem_s[b, c, SUBLANES:, :] = xm[rs, lane_blk(c)]
            _causal_conv(xem_s, xmc_s, b, c, ts, cwm_ref, cbm_ref, act=jax.nn.silu)
        blocks_per_head = DH_IN // LANES
        for hd in range(H_ML):
            xmc = jnp.concatenate([xmc_s[b, hd * blocks_per_head + i] for i in range(blocks_per_head)],
                                  axis=1).astype(BF16)
            qk = _dot(xmc, _wb(wqk_ref, hd))
            qk_s[rs, hd * 2 * DK:hd * 2 * DK + DK] = (qk[:, :DK] * (DK ** -0.5)).astype(BF16)
            qk_s[rs, hd * 2 * DK + DK:(hd + 1) * 2 * DK] = qk[:, DK:].astype(BF16)

        ub = u[rs]
        ig_col = _dot(ub, wicol_ref[...]) + bicol_ref[...]
        lf_col = jax.nn.log_sigmoid(_dot(ub, wfcol_ref[...]) + bfcol_ref[...])
        g_row = _dot_nt(wifrow_ref[...], ub) + bifrow_ref[...]
        ig_row = g_row[:H_ML]
        lf_row = jax.nn.log_sigmoid(g_row[H_ML:])
        bcum_col = jnp.dot(cum_mat, lf_col, precision=lax.Precision.HIGHEST,
                           preferred_element_type=F32)
        bcum_row = lax.dot_general(lf_row, cum_mat, (((1,), (1,)), ((), ())),
                                   precision=lax.Precision.HIGHEST,
                                   preferred_element_type=F32)
        src_row = ig_row - bcum_row
        cmax = ig_col - bcum_col
        k = 1
        while k < chunk:
            cmax = jnp.where(pos_in_chunk >= k, jnp.maximum(cmax, pltpu.roll(cmax, k, axis=0)), cmax)
            k *= 2

        for c in range(n_chunks):
            ls = slice(c * chunk, (c + 1) * chunk)
            gs = slice(b * ts + c * chunk, b * ts + (c + 1) * chunk)
            m_prev = m_ref[b]
            bc = bcum_col[ls]
            inter = bc + m_prev
            m_t = jnp.maximum(inter, bc + cmax[ls])
            w_inter = jnp.exp(inter - m_t)
            floor = jnp.exp(-m_t)
            col_t = bc - m_t
            b_last = bc[chunk - 1:chunk, :]
            m_new = m_t[chunk - 1:chunk, :]
            wk_col = jnp.exp(b_last - bc + ig_col[ls] - m_new)
            decay = jnp.exp(b_last + m_prev - m_new)
            m_ref[b] = m_new
            for hd in range(H_ML):
                q = qk_s[gs, hd * 2 * DK:hd * 2 * DK + DK]
                kk = qk_s[gs, hd * 2 * DK + DK:(hd + 1) * 2 * DK]
                v = v_s[gs, hd * DV:(hd + 1) * DV]
                c_old = c_ref[b, hd]
                n_old = n_ref[b, hd:hd + 1, :]
                dmat = col_t[:, hd:hd + 1] + src_row[hd:hd + 1, ls]
                w_intra = jnp.exp(jnp.where(tri, dmat, -jnp.inf))
                s = _dot_nt(q, kk) * w_intra
                wi = w_inter[:, hd:hd + 1]
                num = _dot(s.astype(BF16), v) + wi * _dot_nt(q, c_old.astype(BF16))
                qn = jnp.sum(q.astype(F32) * n_old.astype(BF16).astype(F32), axis=1, keepdims=True)
                den = jnp.sum(s, axis=1, keepdims=True) + wi * qn
                hm_s[gs, hd * DV:(hd + 1) * DV] = num / jnp.maximum(jnp.abs(den), floor[:, hd:hd + 1])
                kw = kk.astype(F32) * wk_col[:, hd:hd + 1]
                dec = decay[:, hd:hd + 1]
                c_ref[b, hd] = dec * c_old + _dot_tn(v, kw.astype(BF16))
                n_ref[b, hd:hd + 1, :] = dec * n_old + jnp.sum(kw, axis=0, keepdims=True)

    y_b = (jax.nn.sigmoid(_dot(u, _wb(wzm_ref))) * hm_s[...]).astype(BF16)
    gab = _dot(u, _wb(wgab_ref))
    mix = (jax.nn.sigmoid(gab[:, :D_MODEL]) * _dot(ya_s[...], _wb(wpa_ref))
           + jax.nn.sigmoid(gab[:, D_MODEL:]) * _dot(y_b, _wb(wpb_ref)))
    y_ref[...] = (x + _dot(mix.astype(BF16), _wb(wout_ref))).reshape(nb, ts, D_MODEL)

    @pl.when(j == nj - 1)
    def _():
        for c in range(D_RNN // LANES):
            convr_ref[:, :, lane_blk(c)] = xer_s[:, c, tail, :]
        for c in range(D_ML // LANES):
            convm_ref[:, :, lane_blk(c)] = xem_s[:, c, tail, :]

    xer_s[:, :, hist, :] = xer_s[:, :, tail, :]
    xem_s[:, :, hist, :] = xem_s[:, :, tail, :]


def _mlp_kernel(x_ref, gmlp_ref, wup_ref, wdown_ref, gfin_ref, y_ref):
    x = x_ref[...]
    hid = jnp.maximum(_dot(_rmsnorm(x, gmlp_ref[...]).astype(BF16), _wb(wup_ref)), 0.0)
    x = x + _dot((hid * hid).astype(BF16), _wb(wdown_ref))
    y_ref[...] = _rmsnorm(x, gfin_ref[...])


def _resident(shape):
    zeros = (0,) * len(shape)
    return pl.BlockSpec(shape, lambda *_: zeros, pipeline_mode=pl.Buffered(1))


def _mixer(x, conv_r, h, conv_m, c, n, m, weights, *, nb, ts, chunk):
    batch, seq, _ = x.shape
    assert batch % nb == 0 and seq % ts == 0 and ts % chunk == 0 and chunk % SUBLANES == 0
    assert ts % SPAN == 0
    grid = (batch // nb, seq // ts)
    rows = nb * ts
    h = h.reshape(batch, 1, D_RNN)
    m = m.reshape(batch, 1, H_ML)

    def per_batch(shape, **kw):
        nd = len(shape)
        return pl.BlockSpec((nb,) + shape[1:], lambda i, j: (i,) + (0,) * (nd - 1), **kw)

    state_in = (conv_r, h, conv_m, c, n, m)
    in_specs = ([pl.BlockSpec((nb, ts, D_MODEL), lambda i, j: (i, j, 0))]
                + [per_batch(s.shape, pipeline_mode=pl.Buffered(1)) for s in state_in]
                + [_resident(w.shape) for w in weights])
    out_shape = ([jax.ShapeDtypeStruct(x.shape, F32)]
                 + [jax.ShapeDtypeStruct(s.shape, F32) for s in state_in])
    out_specs = ([pl.BlockSpec((nb, ts, D_MODEL), lambda i, j: (i, j, 0))]
                 + [per_batch(s.shape, pipeline_mode=pl.Buffered(1)) for s in state_in])
    scratch = [
        pltpu.VMEM((nb, D_RNN // LANES, SUBLANES + ts, LANES), F32),
        pltpu.VMEM((nb, D_ML // LANES, SUBLANES + ts, LANES), F32),
        pltpu.VMEM((nb, D_RNN // LANES, ts, LANES), F32),
        pltpu.VMEM((nb, RG_BLOCKS, 2, ts, LANES), F32),
        pltpu.VMEM((nb, D_ML // LANES, ts, LANES), F32),
        pltpu.VMEM((rows, D_RNN), BF16),
        pltpu.VMEM((rows, H_ML * 2 * DK), BF16),
        pltpu.VMEM((rows, H_ML * DV), BF16),
        pltpu.VMEM((rows, H_ML * DV), F32),
    ]
    outs = pl.pallas_call(
        functools.partial(_mixer_kernel, nb=nb, ts=ts, chunk=chunk),
        grid=grid,
        in_specs=in_specs,
        out_specs=out_specs,
        out_shape=out_shape,
        scratch_shapes=scratch,
        compiler_params=pltpu.CompilerParams(
            dimension_semantics=("arbitrary", "arbitrary"),
            vmem_limit_bytes=VMEM_LIMIT_BYTES),
    )(x, *state_in, *weights)
    y, conv_r, h, conv_m, c, n, m = outs
    return y, conv_r, h.reshape(batch, D_RNN), conv_m, c, n, m.reshape(batch, H_ML)


def _mlp(x, weights, *, rows):
    batch, seq, _ = x.shape
    total = batch * seq
    assert total % rows == 0
    x2 = x.reshape(total, D_MODEL)
    y = pl.pallas_call(
        _mlp_kernel,
        grid=(total // rows,),
        in_specs=[pl.BlockSpec((rows, D_MODEL), lambda i: (i, 0))]
                 + [_resident(w.shape) for w in weights],
        out_specs=pl.BlockSpec((rows, D_MODEL), lambda i: (i, 0)),
        out_shape=jax.ShapeDtypeStruct((total, D_MODEL), F32),
        compiler_params=pltpu.CompilerParams(
            dimension_semantics=("arbitrary",),
            vmem_limit_bytes=VMEM_LIMIT_BYTES),
    )(x2, *weights)
    return y.reshape(batch, seq, D_MODEL)


def kernel(x_prompt, x_sample, state_rglru_conv, state_rglru_h, state_mlstm_conv, state_mlstm_C, state_mlstm_n, state_mlstm_m, norm_mix, w_in, b_if, conv_rglru_w, conv_rglru_b, rglru_wa, rglru_ba, rglru_wx, rglru_bx, rglru_lambda, conv_mlstm_w, conv_mlstm_b, mlstm_wq, mlstm_wk, mlstm_wv, w_branch_a, w_branch_b, w_out, norm_mlp, w_mlp_up, w_mlp_down, norm_final):
    depth = w_in.shape[0]
    assert depth == 1, "the final norm is fused into the (single) layer's MLP kernel"
    l = 0
    bp = x_prompt.shape[0]
    row = lambda t: t.reshape(1, -1)
    wl = w_in[l]
    o_xm = 2 * D_RNN
    o_zm = o_xm + D_ML
    o_i = o_zm + D_ML
    o_f = o_i + H_ML
    o_g = o_f + H_ML
    mixer_w = (
        row(norm_mix[l]),
        _pack_rows(wl[:, :o_xm]),
        _pack_rows(wl[:, o_xm:o_zm]),
        _pack_rows(wl[:, o_zm:o_i]),
        _pack_rows(wl[:, o_g:]),
        wl[:, o_i:o_f].astype(BF16),
        wl[:, o_f:o_g].astype(BF16),
        wl[:, o_i:o_g].T.astype(BF16),
        row(b_if[l, :H_ML]),
        row(b_if[l, H_ML:]),
        b_if[l].reshape(2 * H_ML, 1),
        conv_rglru_w[l], row(conv_rglru_b[l]),
        _pack_rows(jnp.concatenate([rglru_wa[l], rglru_wx[l]], axis=-1)),
        row(rglru_ba[l]), row(rglru_bx[l]), row(rglru_lambda[l]),
        conv_mlstm_w[l], row(conv_mlstm_b[l]),
        _pack_rows(jnp.concatenate([mlstm_wq[l], mlstm_wk[l]], axis=-1)),
        _pack_rows(mlstm_wv[l]),
        _pack_rows(w_branch_a[l]), _pack_rows(w_branch_b[l]), _pack_rows(w_out[l]),
    )
    outp = _mixer(x_prompt,
                  jnp.zeros((bp, CONV_W - 1, D_RNN), F32), jnp.zeros((bp, D_RNN), F32),
                  jnp.zeros((bp, CONV_W - 1, D_ML), F32), jnp.zeros((bp, H_ML, DV, DK), F32),
                  jnp.zeros((bp, H_ML, DK), F32), jnp.zeros((bp, H_ML), F32),
                  mixer_w, nb=1, ts=256, chunk=128)
    outs = _mixer(x_sample, state_rglru_conv[l], state_rglru_h[l], state_mlstm_conv[l],
                  state_mlstm_C[l], state_mlstm_n[l], state_mlstm_m[l],
                  mixer_w, nb=4, ts=64, chunk=64)
    mlp_w = (row(norm_mlp[l]), _pack_rows(w_mlp_up[l]), _pack_rows(w_mlp_down[l]), row(norm_final))
    y_prompt = _mlp(outp[0], mlp_w, rows=512)
    y_sample = _mlp(outs[0], mlp_w, rows=512)
    p_state = [t[None] for t in outp[1:]]
    s_state = [t[None] for t in outs[1:]]
    return (y_prompt, y_sample, *p_state, *s_state)
```

```python
import functools

import jax
import jax.numpy as jnp
from jax import lax
from jax.experimental import pallas as pl
from jax.experimental.pallas import tpu as pltpu

D_MODEL = 1024
CONV_W = 4
D_RNN = 1024
RG_BLOCKS = 8
RG_BW = D_RNN // RG_BLOCKS
RG_C = 8.0
D_ML = 2 * D_MODEL
H_ML = 8
DH_IN = D_ML // H_ML
DK = 128
DV = 256
D_FF = 4 * D_MODEL
EPS = 1e-6

LANES = 128
SUBLANES = 8
ROW_STRIDE = 4
SPAN = SUBLANES * ROW_STRIDE
VMEM_LIMIT_BYTES = 60 * 1024 * 1024
PACK_ROWS = 256

F32 = jnp.float32
BF16 = jnp.bfloat16


def _dot(a, b):
    return jnp.dot(a, b, preferred_element_type=F32)


def _dot_nt(a, b):
    return lax.dot_general(a, b, (((1,), (1,)), ((), ())), preferred_element_type=F32)


def _dot_tn(a, b):
    return lax.dot_general(a, b, (((0,), (0,)), ((), ())), preferred_element_type=F32)


def _wb(w_ref, *idx):
    return pltpu.bitcast(w_ref[idx] if idx else w_ref[...], BF16)


def _pack_kernel(w_ref, o_ref):
    o_ref[...] = pltpu.bitcast(w_ref[...].astype(BF16), jnp.uint32)


def _pack_rows(w, col_block=0, n=None):
    *lead, k, n_all = w.shape
    n = n_all if n is None else n
    w2 = w.reshape(-1, n_all)
    total = w2.shape[0]
    rb = min(total, PACK_ROWS)
    assert k % 2 == 0 and total % rb == 0 and n % LANES == 0
    packed = pl.pallas_call(
        _pack_kernel,
        grid=(total // rb,),
        in_specs=[pl.BlockSpec((rb, n), lambda i: (i, col_block))],
        out_specs=pl.BlockSpec((rb // 2, n), lambda i: (i, 0)),
        out_shape=jax.ShapeDtypeStruct((total // 2, n), jnp.uint32),
    )(w2)
    return packed.reshape(*lead, k // 2, n)


def _rmsnorm(x, g):
    return x * lax.rsqrt(jnp.mean(x * x, axis=-1, keepdims=True) + EPS) * g


def _strided(start):
    return pl.ds(start, SUBLANES, stride=ROW_STRIDE)


def _causal_conv(xe_ref, out_ref, b, c, t, w_ref, b_ref, act=None):
    cs = slice(c * LANES, (c + 1) * LANES)
    taps = [jnp.broadcast_to(w_ref[k:k + 1, cs], (SUBLANES, LANES)) for k in range(CONV_W)]
    bias = jnp.broadcast_to(b_ref[:, cs], (SUBLANES, LANES))
    for start in range(0, t, SPAN):
        for r in range(ROW_STRIDE):
            y = bias
            for k in range(CONV_W):
                y = y + xe_ref[b, c, _strided(SUBLANES + start + r - k), :] * taps[CONV_W - 1 - k]
            out_ref[b, c, _strided(start + r), :] = y if act is None else act(y)


def _rglru_scan(xc_ref, gate_ref, b, c, t, rate, h0):
    row = lax.broadcasted_iota(jnp.int32, (SUBLANES, LANES), 0)
    rate = jnp.broadcast_to(rate, (SUBLANES, LANES))
    carry = jnp.broadcast_to(h0, (SUBLANES, LANES))
    for start in range(0, t, SPAN):
        h_loc, a_cum = [], []
        for r in range(ROW_STRIDE):
            rows = _strided(start + r)
            xc = xc_ref[b, c, rows, :]
            log_a = jax.nn.sigmoid(gate_ref[b, c, 0, rows, :]) * rate
            a = jnp.exp(log_a)
            gain = jnp.sqrt(-jnp.tanh(log_a) * (a * a + 1.0))
            bt = gain * (jax.nn.sigmoid(gate_ref[b, c, 1, rows, :]) * xc)
            if r == 0:
                h_loc.append(bt)
                a_cum.append(a)
            else:
                h_loc.append(a * h_loc[-1] + bt)
                a_cum.append(a * a_cum[-1])
        h_end, a_end = h_loc[-1], a_cum[-1]
        k = 1
        while k < SUBLANES:
            keep = row >= k
            h_end = h_end + a_end * jnp.where(keep, pltpu.roll(h_end, k, axis=0), 0.0)
            a_end = a_end * jnp.where(keep, pltpu.roll(a_end, k, axis=0), 1.0)
            k *= 2
        h_end = h_end + a_end * carry
        h_in = jnp.where(row >= 1, pltpu.roll(h_end, 1, axis=0), carry)
        for r in range(ROW_STRIDE):
            xc_ref[b, c, _strided(start + r), :] = h_loc[r] + a_cum[r] * h_in
        carry = jnp.broadcast_to(h_end[SUBLANES - 1:SUBLANES, :], (SUBLANES, LANES))
    return carry[0:1, :]


def _mixer_kernel(
        x_ref, convr0_ref, h0_ref, convm0_ref, c0_ref, n0_ref, m0_ref,
        gmix_ref, wxrgr_ref, wxm_ref, wzm_ref, wgab_ref,
        wif_ref, bif_ref,
        cwr_ref, cbr_ref, wax_ref, ba_ref, bx_ref, lam_ref,
        cwm_ref, cbm_ref, wqk_ref, wv_ref, wpa_ref, wpb_ref, wout_ref,
        y_ref, convr_ref, h_ref, convm_ref, c_ref, n_ref, m_ref,
        xer_s, xem_s, xc_s, gate_s, xmc_s, ya_s, qk_s, v_s, hm_s,
        *, nb, ts, chunk):
    j = pl.program_id(1)
    nj = pl.num_programs(1)
    rows = nb * ts
    hist = slice(SUBLANES - (CONV_W - 1), SUBLANES)
    tail = slice(SUBLANES + ts - (CONV_W - 1), SUBLANES + ts)
    lane_blk = lambda c: slice(c * LANES, (c + 1) * LANES)

    @pl.when(j == 0)
    def _():
        for c in range(D_RNN // LANES):
            xer_s[:, c, hist, :] = convr0_ref[:, :, lane_blk(c)]
        for c in range(D_ML // LANES):
            xem_s[:, c, hist, :] = convm0_ref[:, :, lane_blk(c)]
        h_ref[...] = h0_ref[...]
        c_ref[...] = c0_ref[...]
        n_ref[...] = n0_ref[...]
        m_ref[...] = m0_ref[...]

    x = x_ref[...].reshape(rows, D_MODEL)
    u = _rmsnorm(x, gmix_ref[...]).astype(BF16)

    xm = _dot(u, _wb(wxm_ref))
    xmb = xm.astype(BF16)
    for hd in range(H_ML):
        cs = slice(hd * DH_IN, (hd + 1) * DH_IN)
        v_s[:, cs] = _dot(xmb[:, cs], _wb(wv_ref, hd)).astype(BF16)

    n_chunks = ts // chunk
    tpos = lax.broadcasted_iota(jnp.int32, (ts, ts), 0)
    spos = lax.broadcasted_iota(jnp.int32, (ts, ts), 1)
    if n_chunks == 1:
        same_chunk_tri = spos <= tpos
    else:
        same_chunk_tri = (spos <= tpos) & ((tpos // chunk) == (spos // chunk))
    cum_mat = same_chunk_tri.astype(F32)
    tri = (lax.broadcasted_iota(jnp.int32, (chunk, chunk), 1)
           <= lax.broadcasted_iota(jnp.int32, (chunk, chunk), 0))
    pos_in_chunk = lax.broadcasted_iota(jnp.int32, (ts, H_ML), 0) % chunk

    for b in range(nb):
        rs = slice(b * ts, (b + 1) * ts)
        for c in range(D_ML // LANES):
            xem_s[b, c, SUBLANES:, :] = xm[rs, lane_blk(c)]
            _causal_conv(xem_s, xmc_s, b, c, ts, cwm_ref, cbm_ref, act=jax.nn.silu)
        blocks_per_head = DH_IN // LANES
        for hd in range(H_ML):
            xmc = jnp.concatenate([xmc_s[b, hd * blocks_per_head + i] for i in range(blocks_per_head)],
                                  axis=1).astype(BF16)
            qk = _dot(xmc, _wb(wqk_ref, hd))
            qk_s[rs, hd * 2 * DK:hd * 2 * DK + DK] = (qk[:, :DK] * (DK ** -0.5)).astype(BF16)
            qk_s[rs, hd * 2 * DK + DK:(hd + 1) * 2 * DK] = qk[:, DK:].astype(BF16)

        ub = u[rs]
        g_if = _dot(ub, wif_ref[...]) + bif_ref[...]
        ig_col = g_if[:, :H_ML]
        lf_col = jax.nn.log_sigmoid(g_if[:, LANES:LANES + H_ML])
        ig_row = g_if[:, :LANES].T[:H_ML]
        lf_row = jax.nn.log_sigmoid(g_if[:, LANES:].T[:H_ML])
        bcum_col = jnp.dot(cum_mat, lf_col, precision=lax.Precision.HIGHEST,
                           preferred_element_type=F32)
        bcum_row = lax.dot_general(lf_row, cum_mat, (((1,), (1,)), ((), ())),
                                   precision=lax.Precision.HIGHEST,
                                   preferred_element_type=F32)
        src_row = ig_row - bcum_row
        cmax = ig_col - bcum_col
        k = 1
        while k < chunk:
            cmax = jnp.where(pos_in_chunk >= k, jnp.maximum(cmax, pltpu.roll(cmax, k, axis=0)), cmax)
            k *= 2

        for c in range(n_chunks):
            ls = slice(c * chunk, (c + 1) * chunk)
            gs = slice(b * ts + c * chunk, b * ts + (c + 1) * chunk)
            m_prev = m_ref[b]
            bc = bcum_col[ls]
            inter = bc + m_prev
            m_t = jnp.maximum(inter, bc + cmax[ls])
            w_inter = jnp.exp(inter - m_t)
            floor = jnp.exp(-m_t)
            col_t = bc - m_t
            b_last = bc[chunk - 1:chunk, :]
            m_new = m_t[chunk - 1:chunk, :]
            wk_col = jnp.exp(b_last - bc + ig_col[ls] - m_new)
            decay = jnp.exp(b_last + m_prev - m_new)
            m_ref[b] = m_new
            for hd in range(H_ML):
                q = qk_s[gs, hd * 2 * DK:hd * 2 * DK + DK]
                kk = qk_s[gs, hd * 2 * DK + DK:(hd + 1) * 2 * DK]
                v = v_s[gs, hd * DV:(hd + 1) * DV]
                c_old = c_ref[b, hd]
                n_old = n_ref[b, hd:hd + 1, :]
                dmat = col_t[:, hd:hd + 1] + src_row[hd:hd + 1, ls]
                w_intra = jnp.exp(jnp.where(tri, dmat, -jnp.inf))
                s = _dot_nt(q, kk) * w_intra
                wi = w_inter[:, hd:hd + 1]
                num = _dot(s.astype(BF16), v) + wi * _dot_nt(q, c_old.astype(BF16))
                qn = jnp.sum(q.astype(F32) * n_old.astype(BF16).astype(F32), axis=1, keepdims=True)
                den = jnp.sum(s, axis=1, keepdims=True) + wi * qn
                hm_s[gs, hd * DV:(hd + 1) * DV] = num / jnp.maximum(jnp.abs(den), floor[:, hd:hd + 1])
                kw = kk.astype(F32) * wk_col[:, hd:hd + 1]
                dec = decay[:, hd:hd + 1]
                c_ref[b, hd] = dec * c_old + _dot_tn(v, kw.astype(BF16))
                n_ref[b, hd:hd + 1, :] = dec * n_old + jnp.sum(kw, axis=0, keepdims=True)

    xrgr = _dot(u, _wb(wxrgr_ref))
    gr = xrgr[:, D_RNN:]
    rate = -RG_C * jax.nn.softplus(-lam_ref[...])
    for b in range(nb):
        rs = slice(b * ts, (b + 1) * ts)
        for c in range(RG_BLOCKS):
            cs = lane_blk(c)
            xer_s[b, c, SUBLANES:, :] = xrgr[rs, cs]
            _causal_conv(xer_s, xc_s, b, c, ts, cwr_ref, cbr_ref)
            g = _dot(xc_s[b, c].astype(BF16), _wb(wax_ref, c))
            gate_s[b, c, 0] = g[:, :RG_BW] + ba_ref[:, cs]
            gate_s[b, c, 1] = g[:, RG_BW:] + bx_ref[:, cs]
            h_ref[b, :, cs] = _rglru_scan(xc_s, gate_s, b, c, ts, rate[:, cs], h_ref[b, :, cs])
            ya_s[rs, cs] = (xc_s[b, c] * jax.nn.gelu(gr[rs, cs])).astype(BF16)

    y_b = (jax.nn.sigmoid(_dot(u, _wb(wzm_ref))) * hm_s[...]).astype(BF16)
    gab = _dot(u, _wb(wgab_ref))
    mix = (jax.nn.sigmoid(gab[:, :D_MODEL]) * _dot(ya_s[...], _wb(wpa_ref))
           + jax.nn.sigmoid(gab[:, D_MODEL:]) * _dot(y_b, _wb(wpb_ref)))
    y_ref[...] = (x + _dot(mix.astype(BF16), _wb(wout_ref))).reshape(nb, ts, D_MODEL)

    @pl.when(j == nj - 1)
    def _():
        for c in range(D_RNN // LANES):
            convr_ref[:, :, lane_blk(c)] = xer_s[:, c, tail, :]
        for c in range(D_ML // LANES):
            convm_ref[:, :, lane_blk(c)] = xem_s[:, c, tail, :]

    xer_s[:, :, hist, :] = xer_s[:, :, tail, :]
    xem_s[:, :, hist, :] = xem_s[:, :, tail, :]


def _mlp_kernel(x_ref, gmlp_ref, wup_ref, wdown_ref, gfin_ref, y_ref):
    x = x_ref[...]
    hid = jnp.maximum(_dot(_rmsnorm(x, gmlp_ref[...]).astype(BF16), _wb(wup_ref)), 0.0)
    x = x + _dot((hid * hid).astype(BF16), _wb(wdown_ref))
    y_ref[...] = _rmsnorm(x, gfin_ref[...])


def _resident(shape):
    zeros = (0,) * len(shape)
    return pl.BlockSpec(shape, lambda *_: zeros, pipeline_mode=pl.Buffered(1))


def _mixer(x, conv_r, h, conv_m, c, n, m, weights, *, nb, ts, chunk):
    batch, seq, _ = x.shape
    assert batch % nb == 0 and seq % ts == 0 and ts % chunk == 0 and chunk % SUBLANES == 0
    assert ts % SPAN == 0
    grid = (batch // nb, seq // ts)
    rows = nb * ts
    h = h.reshape(batch, 1, D_RNN)
    m = m.reshape(batch, 1, H_ML)

    def per_batch(shape, **kw):
        nd = len(shape)
        return pl.BlockSpec((nb,) + shape[1:], lambda i, j: (i,) + (0,) * (nd - 1), **kw)

    state_in = (conv_r, h, conv_m, c, n, m)
    in_specs = ([pl.BlockSpec((nb, ts, D_MODEL), lambda i, j: (i, j, 0))]
                + [per_batch(s.shape, pipeline_mode=pl.Buffered(1)) for s in state_in]
                + [_resident(w.shape) for w in weights])
    out_shape = ([jax.ShapeDtypeStruct(x.shape, F32)]
                 + [jax.ShapeDtypeStruct(s.shape, F32) for s in state_in])
    out_specs = ([pl.BlockSpec((nb, ts, D_MODEL), lambda i, j: (i, j, 0))]
                 + [per_batch(s.shape, pipeline_mode=pl.Buffered(1)) for s in state_in])
    scratch = [
        pltpu.VMEM((nb, D_RNN // LANES, SUBLANES + ts, LANES), F32),
        pltpu.VMEM((nb, D_ML // LANES, SUBLANES + ts, LANES), F32),
        pltpu.VMEM((nb, D_RNN // LANES, ts, LANES), F32),
        pltpu.VMEM((nb, RG_BLOCKS, 2, ts, LANES), F32),
        pltpu.VMEM((nb, D_ML // LANES, ts, LANES), F32),
        pltpu.VMEM((rows, D_RNN), BF16),
        pltpu.VMEM((rows, H_ML * 2 * DK), BF16),
        pltpu.VMEM((rows, H_ML * DV), BF16),
        pltpu.VMEM((rows, H_ML * DV), F32),
    ]
    outs = pl.pallas_call(
        functools.partial(_mixer_kernel, nb=nb, ts=ts, chunk=chunk),
        grid=grid,
        in_specs=in_specs,
        out_specs=out_specs,
        out_shape=out_shape,
        scratch_shapes=scratch,
        compiler_params=pltpu.CompilerParams(
            dimension_semantics=("arbitrary", "arbitrary"),
            vmem_limit_bytes=VMEM_LIMIT_BYTES),
    )(x, *state_in, *weights)
    y, conv_r, h, conv_m, c, n, m = outs
    return y, conv_r, h.reshape(batch, D_RNN), conv_m, c, n, m.reshape(batch, H_ML)


def _mlp(x, weights, *, rows):
    batch, seq, _ = x.shape
    total = batch * seq
    assert total % rows == 0
    x2 = x.reshape(total, D_MODEL)
    y = pl.pallas_call(
        _mlp_kernel,
        grid=(total // rows,),
        in_specs=[pl.BlockSpec((rows, D_MODEL), lambda i: (i, 0))]
                 + [_resident(w.shape) for w in weights],
        out_specs=pl.BlockSpec((rows, D_MODEL), lambda i: (i, 0)),
        out_shape=jax.ShapeDtypeStruct((total, D_MODEL), F32),
        compiler_params=pltpu.CompilerParams(
            dimension_semantics=("arbitrary",),
            vmem_limit_bytes=VMEM_LIMIT_BYTES),
    )(x2, *weights)
    return y.reshape(batch, seq, D_MODEL)


def kernel(x_prompt, x_sample, state_rglru_conv, state_rglru_h, state_mlstm_conv, state_mlstm_C, state_mlstm_n, state_mlstm_m, norm_mix, w_in, b_if, conv_rglru_w, conv_rglru_b, rglru_wa, rglru_ba, rglru_wx, rglru_bx, rglru_lambda, conv_mlstm_w, conv_mlstm_b, mlstm_wq, mlstm_wk, mlstm_wv, w_branch_a, w_branch_b, w_out, norm_mlp, w_mlp_up, w_mlp_down, norm_final):
    depth = w_in.shape[0]
    assert depth == 1, "the final norm is fused into the (single) layer's MLP kernel"
    l = 0
    bp = x_prompt.shape[0]
    row = lambda t: t.reshape(1, -1)
    wl = w_in[l]
    o_xm = 2 * D_RNN
    o_zm = o_xm + D_ML
    o_i = o_zm + D_ML
    o_f = o_i + H_ML
    o_g = o_f + H_ML
    assert o_xm == D_ML, "the first three column groups of w_in are packed as equal-width blocks"
    lane_pad = lambda t: jnp.pad(t, ((0, 0), (0, LANES - t.shape[1])))
    mixer_w = (
        row(norm_mix[l]),
        _pack_rows(wl, 0, D_ML),
        _pack_rows(wl, 1, D_ML),
        _pack_rows(wl, 2, D_ML),
        _pack_rows(wl[:, o_g:]),
        jnp.concatenate([lane_pad(wl[:, o_i:o_f]), lane_pad(wl[:, o_f:o_g])], axis=1).astype(BF16),
        jnp.concatenate([lane_pad(row(b_if[l, :H_ML])), lane_pad(row(b_if[l, H_ML:]))], axis=1),
        conv_rglru_w[l], row(conv_rglru_b[l]),
        _pack_rows(jnp.concatenate([rglru_wa[l], rglru_wx[l]], axis=-1)),
        row(rglru_ba[l]), row(rglru_bx[l]), row(rglru_lambda[l]),
        conv_mlstm_w[l], row(conv_mlstm_b[l]),
        _pack_rows(jnp.concatenate([mlstm_wq[l], mlstm_wk[l]], axis=-1)),
        _pack_rows(mlstm_wv[l]),
        _pack_rows(w_branch_a[l]), _pack_rows(w_branch_b[l]), _pack_rows(w_out[l]),
    )
    outp = _mixer(x_prompt,
                  jnp.zeros((bp, CONV_W - 1, D_RNN), F32), jnp.zeros((bp, D_RNN), F32),
                  jnp.zeros((bp, CONV_W - 1, D_ML), F32), jnp.zeros((bp, H_ML, DV, DK), F32),
                  jnp.zeros((bp, H_ML, DK), F32), jnp.zeros((bp, H_ML), F32),
                  mixer_w, nb=1, ts=256, chunk=128)
    outs = _mixer(x_sample, state_rglru_conv[l], state_rglru_h[l], state_mlstm_conv[l],
                  state_mlstm_C[l], state_mlstm_n[l], state_mlstm_m[l],
                  mixer_w, nb=4, ts=64, chunk=64)
    mlp_w = (row(norm_mlp[l]), _pack_rows(w_mlp_up[l]), _pack_rows(w_mlp_down[l]), row(norm_final))
    y_prompt = _mlp(outp[0], mlp_w, rows=512)
    y_sample = _mlp(outs[0], mlp_w, rows=512)
    p_state = [t[None] for t in outp[1:]]
    s_state = [t[None] for t in outs[1:]]
    return (y_prompt, y_sample, *p_state, *s_state)
```

```python
import functools

import jax
import jax.numpy as jnp
from jax import lax
from jax.experimental import pallas as pl
from jax.experimental.pallas import tpu as pltpu

D_MODEL = 1024
CONV_W = 4
D_RNN = 1024
RG_BLOCKS = 8
RG_BW = D_RNN // RG_BLOCKS
RG_C = 8.0
D_ML = 2 * D_MODEL
H_ML = 8
DH_IN = D_ML // H_ML
DK = 128
DV = 256
D_FF = 4 * D_MODEL
EPS = 1e-6

LANES = 128
SUBLANES = 8
ROW_STRIDE = 2
SPAN = SUBLANES * ROW_STRIDE
VMEM_LIMIT_BYTES = 60 * 1024 * 1024
PACK_ROWS = 256

F32 = jnp.float32
BF16 = jnp.bfloat16


def _dot(a, b):
    return jnp.dot(a, b, preferred_element_type=F32)


def _dot_nt(a, b):
    return lax.dot_general(a, b, (((1,), (1,)), ((), ())), preferred_element_type=F32)


def _dot_tn(a, b):
    return lax.dot_general(a, b, (((0,), (0,)), ((), ())), preferred_element_type=F32)


def _wb(w_ref, *idx):
    return pltpu.bitcast(w_ref[idx] if idx else w_ref[...], BF16)


def _pack_kernel(w_ref, o_ref):
    o_ref[...] = pltpu.bitcast(w_ref[...].astype(BF16), jnp.uint32)


def _pack_rows(w, col_block=0, n=None):
    *lead, k, n_all = w.shape
    n = n_all if n is None else n
    w2 = w.reshape(-1, n_all)
    total = w2.shape[0]
    rb = min(total, PACK_ROWS)
    assert k % 2 == 0 and total % rb == 0 and n % LANES == 0
    packed = pl.pallas_call(
        _pack_kernel,
        grid=(total // rb,),
        in_specs=[pl.BlockSpec((rb, n), lambda i: (i, col_block))],
        out_specs=pl.BlockSpec((rb // 2, n), lambda i: (i, 0)),
        out_shape=jax.ShapeDtypeStruct((total // 2, n), jnp.uint32),
    )(w2)
    return packed.reshape(*lead, k // 2, n)


def _rmsnorm(x, g):
    return x * lax.rsqrt(jnp.mean(x * x, axis=-1, keepdims=True) + EPS) * g


def _strided(start):
    return pl.ds(start, SUBLANES, stride=ROW_STRIDE)


def _causal_conv(xe_ref, out_ref, b, c, t, w_ref, b_ref, act=None):
    cs = slice(c * LANES, (c + 1) * LANES)
    taps = [jnp.broadcast_to(w_ref[k:k + 1, cs], (SUBLANES, LANES)) for k in range(CONV_W)]
    bias = jnp.broadcast_to(b_ref[:, cs], (SUBLANES, LANES))
    for start in range(0, t, SPAN):
        for r in range(ROW_STRIDE):
            y = bias
            for k in range(CONV_W):
                y = y + xe_ref[b, c, _strided(SUBLANES + start + r - k), :] * taps[CONV_W - 1 - k]
            out_ref[b, c, _strided(start + r), :] = y if act is None else act(y)


def _rglru_scan(xc_ref, gate_ref, b, c, t, rate, h0):
    row = lax.broadcasted_iota(jnp.int32, (SUBLANES, LANES), 0)
    rate = jnp.broadcast_to(rate, (SUBLANES, LANES))
    carry = jnp.broadcast_to(h0, (SUBLANES, LANES))
    for start in range(0, t, SPAN):
        h_loc, a_cum = [], []
        for r in range(ROW_STRIDE):
            rows = _strided(start + r)
            xc = xc_ref[b, c, rows, :]
            log_a = jax.nn.sigmoid(gate_ref[b, c, 0, rows, :]) * rate
            a = jnp.exp(log_a)
            gain = jnp.sqrt(-jnp.tanh(log_a) * (a * a + 1.0))
            bt = gain * (jax.nn.sigmoid(gate_ref[b, c, 1, rows, :]) * xc)
            if r == 0:
                h_loc.append(bt)
                a_cum.append(a)
            else:
                h_loc.append(a * h_loc[-1] + bt)
                a_cum.append(a * a_cum[-1])
        h_end, a_end = h_loc[-1], a_cum[-1]
        k = 1
        while k < SUBLANES:
            keep = row >= k
            h_end = h_end + a_end * jnp.where(keep, pltpu.roll(h_end, k, axis=0), 0.0)
            a_end = a_end * jnp.where(keep, pltpu.roll(a_end, k, axis=0), 1.0)
            k *= 2
        h_end = h_end + a_end * carry
        h_in = jnp.where(row >= 1, pltpu.roll(h_end, 1, axis=0), carry)
        for r in range(ROW_STRIDE):
            xc_ref[b, c, _strided(start + r), :] = h_loc[r] + a_cum[r] * h_in
        carry = jnp.broadcast_to(h_end[SUBLANES - 1:SUBLANES, :], (SUBLANES, LANES))
    return carry[0:1, :]


def _mixer_kernel(
        x_ref, convr0_ref, h0_ref, convm0_ref, c0_ref, n0_ref, m0_ref,
        gmix_ref, wxrgr_ref, wxm_ref, wzm_ref, wgab_ref,
        wif_ref, bif_ref,
        cwr_ref, cbr_ref, wax_ref, ba_ref, bx_ref, lam_ref,
        cwm_ref, cbm_ref, wqk_ref, wv_ref, wpa_ref, wpb_ref, wout_ref,
        y_ref, convr_ref, h_ref, convm_ref, c_ref, n_ref, m_ref,
        xer_s, xem_s, xc_s, gate_s, xmc_s, gr_s, xmb_s, ya_s, qk_s, v_s, hm_s,
        *, nb, ts, chunk):
    j = pl.program_id(1)
    nj = pl.num_programs(1)
    rows = nb * ts
    hist = slice(SUBLANES - (CONV_W - 1), SUBLANES)
    tail = slice(SUBLANES + ts - (CONV_W - 1), SUBLANES + ts)
    lane_blk = lambda c: slice(c * LANES, (c + 1) * LANES)

    @pl.when(j == 0)
    def _():
        for c in range(D_RNN // LANES):
            xer_s[:, c, hist, :] = convr0_ref[:, :, lane_blk(c)]
        for c in range(D_ML // LANES):
            xem_s[:, c, hist, :] = convm0_ref[:, :, lane_blk(c)]
        h_ref[...] = h0_ref[...]
        c_ref[...] = c0_ref[...]
        n_ref[...] = n0_ref[...]
        m_ref[...] = m0_ref[...]

    u = _rmsnorm(x_ref[...].reshape(rows, D_MODEL), gmix_ref[...]).astype(BF16)

    n_chunks = ts // chunk
    tpos = lax.broadcasted_iota(jnp.int32, (ts, ts), 0)
    spos = lax.broadcasted_iota(jnp.int32, (ts, ts), 1)
    if n_chunks == 1:
        same_chunk_tri = spos <= tpos
    else:
        same_chunk_tri = (spos <= tpos) & ((tpos // chunk) == (spos // chunk))
    cum_mat = same_chunk_tri.astype(F32)
    tri = (lax.broadcasted_iota(jnp.int32, (chunk, chunk), 1)
           <= lax.broadcasted_iota(jnp.int32, (chunk, chunk), 0))
    pos_in_chunk = lax.broadcasted_iota(jnp.int32, (ts, H_ML), 0) % chunk

    g_if_all = _dot(u, wif_ref[...]) + bif_ref[...]

    xm = _dot(u, _wb(wxm_ref))
    xmb_s[...] = xm.astype(BF16)
    for b in range(nb):
        for c in range(D_ML // LANES):
            xem_s[b, c, SUBLANES:, :] = xm[b * ts:(b + 1) * ts, lane_blk(c)]
    for hd in range(H_ML):
        cs = slice(hd * DH_IN, (hd + 1) * DH_IN)
        v_s[:, cs] = _dot(xmb_s[:, cs], _wb(wv_ref, hd)).astype(BF16)

    gates = []
    for b in range(nb):
        g_if = g_if_all[b * ts:(b + 1) * ts]
        ig_col = g_if[:, :H_ML]
        lf_col = jax.nn.log_sigmoid(g_if[:, LANES:LANES + H_ML])
        ig_row = g_if[:, :LANES].T[:H_ML]
        lf_row = jax.nn.log_sigmoid(g_if[:, LANES:].T[:H_ML])
        bcum_col = jnp.dot(cum_mat, lf_col, precision=lax.Precision.HIGHEST,
                           preferred_element_type=F32)
        bcum_row = lax.dot_general(lf_row, cum_mat, (((1,), (1,)), ((), ())),
                                   precision=lax.Precision.HIGHEST,
                                   preferred_element_type=F32)
        src_row = ig_row - bcum_row
        cmax = ig_col - bcum_col
        k = 1
        while k < chunk:
            cmax = jnp.where(pos_in_chunk >= k, jnp.maximum(cmax, pltpu.roll(cmax, k, axis=0)), cmax)
            k *= 2

        per_chunk = []
        m_prev = m_ref[b]
        for c in range(n_chunks):
            ls = slice(c * chunk, (c + 1) * chunk)
            bc = bcum_col[ls]
            inter = bc + m_prev
            m_t = jnp.maximum(inter, bc + cmax[ls])
            b_last = bc[chunk - 1:chunk, :]
            m_new = m_t[chunk - 1:chunk, :]
            per_chunk.append((
                bc - m_t,
                src_row[:, ls],
                jnp.exp(inter - m_t),
                jnp.exp(-m_t),
                jnp.exp(b_last - bc + ig_col[ls] - m_new),
                jnp.exp(b_last + m_prev - m_new)))
            m_prev = m_new
        m_ref[b] = m_prev
        gates.append(per_chunk)

    for b in range(nb):
        rs = slice(b * ts, (b + 1) * ts)
        for c in range(D_ML // LANES):
            _causal_conv(xem_s, xmc_s, b, c, ts, cwm_ref, cbm_ref, act=jax.nn.silu)
        blocks_per_head = DH_IN // LANES
        for hd in range(H_ML):
            xmc = jnp.concatenate([xmc_s[b, hd * blocks_per_head + i] for i in range(blocks_per_head)],
                                  axis=1).astype(BF16)
            qk = _dot(xmc, _wb(wqk_ref, hd))
            qk_s[rs, hd * 2 * DK:hd * 2 * DK + DK] = (qk[:, :DK] * (DK ** -0.5)).astype(BF16)
            qk_s[rs, hd * 2 * DK + DK:(hd + 1) * 2 * DK] = qk[:, DK:].astype(BF16)

        for c in range(n_chunks):
            gs = slice(b * ts + c * chunk, b * ts + (c + 1) * chunk)
            col_t, src_rows, w_inter, floor, wk_col, decay = gates[b][c]
            for hd in range(H_ML):
                q = qk_s[gs, hd * 2 * DK:hd * 2 * DK + DK]
                kk = qk_s[gs, hd * 2 * DK + DK:(hd + 1) * 2 * DK]
                v = v_s[gs, hd * DV:(hd + 1) * DV]
                c_old = c_ref[b, hd]
                n_old = n_ref[b, hd:hd + 1, :]
                dmat = col_t[:, hd:hd + 1] + src_rows[hd:hd + 1, :]
                w_intra = jnp.exp(jnp.where(tri, dmat, -jnp.inf))
                s = _dot_nt(q, kk) * w_intra
                wi = w_inter[:, hd:hd + 1]
                num = _dot(s.astype(BF16), v) + wi * _dot_nt(q, c_old.astype(BF16))
                qn = jnp.sum(q.astype(F32) * n_old.astype(BF16).astype(F32), axis=1, keepdims=True)
                den = jnp.sum(s, axis=1, keepdims=True) + wi * qn
                hm_s[gs, hd * DV:(hd + 1) * DV] = num / jnp.maximum(jnp.abs(den), floor[:, hd:hd + 1])
                kw = kk.astype(F32) * wk_col[:, hd:hd + 1]
                dec = decay[:, hd:hd + 1]
                c_ref[b, hd] = dec * c_old + _dot_tn(v, kw.astype(BF16))
                n_ref[b, hd:hd + 1, :] = dec * n_old + jnp.sum(kw, axis=0, keepdims=True)

    xrgr = _dot(u, _wb(wxrgr_ref))
    gr_s[...] = xrgr[:, D_RNN:]
    for b in range(nb):
        for c in range(RG_BLOCKS):
            xer_s[b, c, SUBLANES:, :] = xrgr[b * ts:(b + 1) * ts, lane_blk(c)]
    rate = -RG_C * jax.nn.softplus(-lam_ref[...])
    for b in range(nb):
        rs = slice(b * ts, (b + 1) * ts)
        for c in range(RG_BLOCKS):
            cs = lane_blk(c)
            _causal_conv(xer_s, xc_s, b, c, ts, cwr_ref, cbr_ref)
            g = _dot(xc_s[b, c].astype(BF16), _wb(wax_ref, c))
            gate_s[b, c, 0] = g[:, :RG_BW] + ba_ref[:, cs]
            gate_s[b, c, 1] = g[:, RG_BW:] + bx_ref[:, cs]
            h_ref[b, :, cs] = _rglru_scan(xc_s, gate_s, b, c, ts, rate[:, cs], h_ref[b, :, cs])
            ya_s[rs, cs] = (xc_s[b, c] * jax.nn.gelu(gr_s[rs, cs])).astype(BF16)

    y_b = (jax.nn.sigmoid(_dot(u, _wb(wzm_ref))) * hm_s[...]).astype(BF16)
    gab = _dot(u, _wb(wgab_ref))
    mix = (jax.nn.sigmoid(gab[:, :D_MODEL]) * _dot(ya_s[...], _wb(wpa_ref))
           + jax.nn.sigmoid(gab[:, D_MODEL:]) * _dot(y_b, _wb(wpb_ref)))
    y_ref[...] = x_ref[...] + _dot(mix.astype(BF16), _wb(wout_ref)).reshape(nb, ts, D_MODEL)

    @pl.when(j == nj - 1)
    def _():
        for c in range(D_RNN // LANES):
            convr_ref[:, :, lane_blk(c)] = xer_s[:, c, tail, :]
        for c in range(D_ML // LANES):
            convm_ref[:, :, lane_blk(c)] = xem_s[:, c, tail, :]

    xer_s[:, :, hist, :] = xer_s[:, :, tail, :]
    xem_s[:, :, hist, :] = xem_s[:, :, tail, :]


def _mlp_kernel(x_ref, gmlp_ref, wup_ref, wdown_ref, gfin_ref, y_ref):
    x = x_ref[...]
    hid = jnp.maximum(_dot(_rmsnorm(x, gmlp_ref[...]).astype(BF16), _wb(wup_ref)), 0.0)
    x = x + _dot((hid * hid).astype(BF16), _wb(wdown_ref))
    y_ref[...] = _rmsnorm(x, gfin_ref[...])


def _resident(shape):
    zeros = (0,) * len(shape)
    return pl.BlockSpec(shape, lambda *_: zeros, pipeline_mode=pl.Buffered(1))


def _mixer(x, conv_r, h, conv_m, c, n, m, weights, *, nb, ts, chunk):
    batch, seq, _ = x.shape
    assert batch % nb == 0 and seq % ts == 0 and ts % chunk == 0 and chunk % SUBLANES == 0
    assert ts % SPAN == 0
    grid = (batch // nb, seq // ts)
    rows = nb * ts
    h = h.reshape(batch, 1, D_RNN)
    m = m.reshape(batch, 1, H_ML)

    def per_batch(shape, **kw):
        nd = len(shape)
        return pl.BlockSpec((nb,) + shape[1:], lambda i, j: (i,) + (0,) * (nd - 1), **kw)

    state_in = (conv_r, h, conv_m, c, n, m)
    in_specs = ([pl.BlockSpec((nb, ts, D_MODEL), lambda i, j: (i, j, 0))]
                + [per_batch(s.shape, pipeline_mode=pl.Buffered(1)) for s in state_in]
                + [_resident(w.shape) for w in weights])
    out_shape = ([jax.ShapeDtypeStruct(x.shape, F32)]
                 + [jax.ShapeDtypeStruct(s.shape, F32) for s in state_in])
    out_specs = ([pl.BlockSpec((nb, ts, D_MODEL), lambda i, j: (i, j, 0))]
                 + [per_batch(s.shape, pipeline_mode=pl.Buffered(1)) for s in state_in])
    scratch = [
        pltpu.VMEM((nb, D_RNN // LANES, SUBLANES + ts, LANES), F32),
        pltpu.VMEM((nb, D_ML // LANES, SUBLANES + ts, LANES), F32),
        pltpu.VMEM((nb, D_RNN // LANES, ts, LANES), F32),
        pltpu.VMEM((nb, RG_BLOCKS, 2, ts, LANES), F32),
        pltpu.VMEM((nb, D_ML // LANES, ts, LANES), F32),
        pltpu.VMEM((rows, D_RNN), F32),
        pltpu.VMEM((rows, D_ML), BF16),
        pltpu.VMEM((rows, D_RNN), BF16),
        pltpu.VMEM((rows, H_ML * 2 * DK), BF16),
        pltpu.VMEM((rows, H_ML * DV), BF16),
        pltpu.VMEM((rows, H_ML * DV), F32),
    ]
    outs = pl.pallas_call(
        functools.partial(_mixer_kernel, nb=nb, ts=ts, chunk=chunk),
        grid=grid,
        in_specs=in_specs,
        out_specs=out_specs,
        out_shape=out_shape,
        scratch_shapes=scratch,
        compiler_params=pltpu.CompilerParams(
            dimension_semantics=("arbitrary", "arbitrary"),
            vmem_limit_bytes=VMEM_LIMIT_BYTES),
    )(x, *state_in, *weights)
    y, conv_r, h, conv_m, c, n, m = outs
    return y, conv_r, h.reshape(batch, D_RNN), conv_m, c, n, m.reshape(batch, H_ML)


def _mlp(x, weights, *, rows):
    batch, seq, _ = x.shape
    total = batch * seq
    assert total % rows == 0
    x2 = x.reshape(total, D_MODEL)
    y = pl.pallas_call(
        _mlp_kernel,
        grid=(total // rows,),
        in_specs=[pl.BlockSpec((rows, D_MODEL), lambda i: (i, 0))]
                 + [_resident(w.shape) for w in weights],
        out_specs=pl.BlockSpec((rows, D_MODEL), lambda i: (i, 0)),
        out_shape=jax.ShapeDtypeStruct((total, D_MODEL), F32),
        compiler_params=pltpu.CompilerParams(
            dimension_semantics=("arbitrary",),
            vmem_limit_bytes=VMEM_LIMIT_BYTES),
    )(x2, *weights)
    return y.reshape(batch, seq, D_MODEL)


def kernel(x_prompt, x_sample, state_rglru_conv, state_rglru_h, state_mlstm_conv, state_mlstm_C, state_mlstm_n, state_mlstm_m, norm_mix, w_in, b_if, conv_rglru_w, conv_rglru_b, rglru_wa, rglru_ba, rglru_wx, rglru_bx, rglru_lambda, conv_mlstm_w, conv_mlstm_b, mlstm_wq, mlstm_wk, mlstm_wv, w_branch_a, w_branch_b, w_out, norm_mlp, w_mlp_up, w_mlp_down, norm_final):
    depth = w_in.shape[0]
    assert depth == 1, "the final norm is fused into the (single) layer's MLP kernel"
    l = 0
    bp = x_prompt.shape[0]
    row = lambda t: t.reshape(1, -1)
    wl = w_in[l]
    o_xm = 2 * D_RNN
    o_zm = o_xm + D_ML
    o_i = o_zm + D_ML
    o_f = o_i + H_ML
    o_g = o_f + H_ML
    assert o_xm == D_ML, "the first three column groups of w_in are packed as equal-width blocks"
    lane_pad = lambda t: jnp.pad(t, ((0, 0), (0, LANES - t.shape[1])))
    mixer_w = (
        row(norm_mix[l]),
        _pack_rows(wl, 0, D_ML),
        _pack_rows(wl, 1, D_ML),
        _pack_rows(wl, 2, D_ML),
        _pack_rows(wl[:, o_g:]),
        jnp.concatenate([lane_pad(wl[:, o_i:o_f]), lane_pad(wl[:, o_f:o_g])], axis=1).astype(BF16),
        jnp.concatenate([lane_pad(row(b_if[l, :H_ML])), lane_pad(row(b_if[l, H_ML:]))], axis=1),
        conv_rglru_w[l], row(conv_rglru_b[l]),
        _pack_rows(jnp.concatenate([rglru_wa[l], rglru_wx[l]], axis=-1)),
        row(rglru_ba[l]), row(rglru_bx[l]), row(rglru_lambda[l]),
        conv_mlstm_w[l], row(conv_mlstm_b[l]),
        _pack_rows(jnp.concatenate([mlstm_wq[l], mlstm_wk[l]], axis=-1)),
        _pack_rows(mlstm_wv[l]),
        _pack_rows(w_branch_a[l]), _pack_rows(w_branch_b[l]), _pack_rows(w_out[l]),
    )
    outp = _mixer(x_prompt,
                  jnp.zeros((bp, CONV_W - 1, D_RNN), F32), jnp.zeros((bp, D_RNN), F32),
                  jnp.zeros((bp, CONV_W - 1, D_ML), F32), jnp.zeros((bp, H_ML, DV, DK), F32),
                  jnp.zeros((bp, H_ML, DK), F32), jnp.zeros((bp, H_ML), F32),
                  mixer_w, nb=1, ts=256, chunk=128)
    outs = _mixer(x_sample, state_rglru_conv[l], state_rglru_h[l], state_mlstm_conv[l],
                  state_mlstm_C[l], state_mlstm_n[l], state_mlstm_m[l],
                  mixer_w, nb=4, ts=64, chunk=64)
    mlp_w = (row(norm_mlp[l]), _pack_rows(w_mlp_up[l]), _pack_rows(w_mlp_down[l]), row(norm_final))
    y_prompt = _mlp(outp[0], mlp_w, rows=512)
    y_sample = _mlp(outs[0], mlp_w, rows=512)
    p_state = [t[None] for t in outp[1:]]
    s_state = [t[None] for t in outs[1:]]
    return (y_prompt, y_sample, *p_state, *s_state)
```

```python
import functools

import jax
import jax.numpy as jnp
from jax import lax
from jax.experimental import pallas as pl
from jax.experimental.pallas import tpu as pltpu

D_MODEL = 1024
CONV_W = 4
D_RNN = 1024
RG_BLOCKS = 8
RG_BW = D_RNN // RG_BLOCKS
RG_C = 8.0
D_ML = 2 * D_MODEL
H_ML = 8
DH_IN = D_ML // H_ML
DK = 128
DV = 256
D_FF = 4 * D_MODEL
EPS = 1e-6

LANES = 128
SUBLANES = 8
ROW_STRIDE = 4
SPAN = SUBLANES * ROW_STRIDE
VMEM_LIMIT_BYTES = 60 * 1024 * 1024
PACK_ROWS = 256

F32 = jnp.float32
BF16 = jnp.bfloat16


def _dot(a, b):
    return jnp.dot(a, b, preferred_element_type=F32)


def _dot_nt(a, b):
    return lax.dot_general(a, b, (((1,), (1,)), ((), ())), preferred_element_type=F32)


def _dot_tn(a, b):
    return lax.dot_general(a, b, (((0,), (0,)), ((), ())), preferred_element_type=F32)


def _wb(w_ref, *idx):
    return pltpu.bitcast(w_ref[idx] if idx else w_ref[...], BF16)


def _pack_kernel(w_ref, o_ref):
    o_ref[...] = pltpu.bitcast(w_ref[...].astype(BF16), jnp.uint32)


def _pack_rows(w, col_block=0, n=None):
    *lead, k, n_all = w.shape
    n = n_all if n is None else n
    w2 = w.reshape(-1, n_all)
    total = w2.shape[0]
    rb = min(total, PACK_ROWS)
    assert k % 2 == 0 and total % rb == 0 and n % LANES == 0
    packed = pl.pallas_call(
        _pack_kernel,
        grid=(total // rb,),
        in_specs=[pl.BlockSpec((rb, n), lambda i: (i, col_block))],
        out_specs=pl.BlockSpec((rb // 2, n), lambda i: (i, 0)),
        out_shape=jax.ShapeDtypeStruct((total // 2, n), jnp.uint32),
    )(w2)
    return packed.reshape(*lead, k // 2, n)


def _rmsnorm(x, g):
    return x * lax.rsqrt(jnp.mean(x * x, axis=-1, keepdims=True) + EPS) * g


def _strided(start):
    return pl.ds(start, SUBLANES, stride=ROW_STRIDE)


def _causal_conv(xe_ref, out_ref, b, c, t, w_ref, b_ref, act=None):
    cs = slice(c * LANES, (c + 1) * LANES)
    taps = [jnp.broadcast_to(w_ref[k:k + 1, cs], (SUBLANES, LANES)) for k in range(CONV_W)]
    bias = jnp.broadcast_to(b_ref[:, cs], (SUBLANES, LANES))
    for start in range(0, t, SPAN):
        for r in range(ROW_STRIDE):
            y = bias
            for k in range(CONV_W):
                y = y + xe_ref[b, c, _strided(SUBLANES + start + r - k), :] * taps[CONV_W - 1 - k]
            out_ref[b, c, _strided(start + r), :] = y if act is None else act(y)


def _rglru_scan(xc_ref, gate_ref, b, c, t, rate, h0):
    row = lax.broadcasted_iota(jnp.int32, (SUBLANES, LANES), 0)
    rate = jnp.broadcast_to(rate, (SUBLANES, LANES))
    carry = jnp.broadcast_to(h0, (SUBLANES, LANES))
    for start in range(0, t, SPAN):
        h_loc, a_cum = [], []
        for r in range(ROW_STRIDE):
            rows = _strided(start + r)
            xc = xc_ref[b, c, rows, :]
            log_a = jax.nn.sigmoid(gate_ref[b, c, 0, rows, :]) * rate
            a = jnp.exp(log_a)
            gain = jnp.sqrt(-jnp.tanh(log_a) * (a * a + 1.0))
            bt = gain * (jax.nn.sigmoid(gate_ref[b, c, 1, rows, :]) * xc)
            if r == 0:
                h_loc.append(bt)
                a_cum.append(a)
            else:
                h_loc.append(a * h_loc[-1] + bt)
                a_cum.append(a * a_cum[-1])
        h_end, a_end = h_loc[-1], a_cum[-1]
        k = 1
        while k < SUBLANES:
            keep = row >= k
            h_end = h_end + a_end * jnp.where(keep, pltpu.roll(h_end, k, axis=0), 0.0)
            a_end = a_end * jnp.where(keep, pltpu.roll(a_end, k, axis=0), 1.0)
            k *= 2
        h_end = h_end + a_end * carry
        h_in = jnp.where(row >= 1, pltpu.roll(h_end, 1, axis=0), carry)
        for r in range(ROW_STRIDE):
            xc_ref[b, c, _strided(start + r), :] = h_loc[r] + a_cum[r] * h_in
        carry = jnp.broadcast_to(h_end[SUBLANES - 1:SUBLANES, :], (SUBLANES, LANES))
    return carry[0:1, :]


def _mixer_kernel(
        x_ref, convr0_ref, h0_ref, convm0_ref, c0_ref, n0_ref, m0_ref,
        gmix_ref, wxrgr_ref, wxm_ref, wzm_ref, wgab_ref,
        wif_ref, bif_ref,
        cwr_ref, cbr_ref, wax_ref, ba_ref, bx_ref, lam_ref,
        cwm_ref, cbm_ref, wqk_ref, wv_ref, wpa_ref, wpb_ref, wout_ref,
        y_ref, convr_ref, h_ref, convm_ref, c_ref, n_ref, m_ref,
        xer_s, xem_s, xc_s, gate_s, xmc_s, gr_s, xmb_s, ya_s, qk_s, v_s, hm_s,
        *, nb, ts, chunk):
    j = pl.program_id(1)
    nj = pl.num_programs(1)
    rows = nb * ts
    hist = slice(SUBLANES - (CONV_W - 1), SUBLANES)
    tail = slice(SUBLANES + ts - (CONV_W - 1), SUBLANES + ts)
    lane_blk = lambda c: slice(c * LANES, (c + 1) * LANES)

    @pl.when(j == 0)
    def _():
        for c in range(D_RNN // LANES):
            xer_s[:, c, hist, :] = convr0_ref[:, :, lane_blk(c)]
        for c in range(D_ML // LANES):
            xem_s[:, c, hist, :] = convm0_ref[:, :, lane_blk(c)]
        h_ref[...] = h0_ref[...]
        c_ref[...] = c0_ref[...]
        n_ref[...] = n0_ref[...]
        m_ref[...] = m0_ref[...]

    u = _rmsnorm(x_ref[...].reshape(rows, D_MODEL), gmix_ref[...]).astype(BF16)

    n_chunks = ts // chunk
    tpos = lax.broadcasted_iota(jnp.int32, (ts, ts), 0)
    spos = lax.broadcasted_iota(jnp.int32, (ts, ts), 1)
    if n_chunks == 1:
        same_chunk_tri = spos <= tpos
    else:
        same_chunk_tri = (spos <= tpos) & ((tpos // chunk) == (spos // chunk))
    cum_mat = same_chunk_tri.astype(F32)
    tri = (lax.broadcasted_iota(jnp.int32, (chunk, chunk), 1)
           <= lax.broadcasted_iota(jnp.int32, (chunk, chunk), 0))
    pos_in_chunk = lax.broadcasted_iota(jnp.int32, (ts, H_ML), 0) % chunk

    g_if_all = _dot(u, wif_ref[...]) + bif_ref[...]

    xm = _dot(u, _wb(wxm_ref))
    xmb_s[...] = xm.astype(BF16)
    for b in range(nb):
        for c in range(D_ML // LANES):
            xem_s[b, c, SUBLANES:, :] = xm[b * ts:(b + 1) * ts, lane_blk(c)]
    for hd in range(H_ML):
        cs = slice(hd * DH_IN, (hd + 1) * DH_IN)
        v_s[:, cs] = _dot(xmb_s[:, cs], _wb(wv_ref, hd)).astype(BF16)

    gates = []
    for b in range(nb):
        g_if = g_if_all[b * ts:(b + 1) * ts]
        ig_col = g_if[:, :H_ML]
        lf_col = jax.nn.log_sigmoid(g_if[:, LANES:LANES + H_ML])
        ig_row = g_if[:, :LANES].T[:H_ML]
        lf_row = jax.nn.log_sigmoid(g_if[:, LANES:].T[:H_ML])
        bcum_col = jnp.dot(cum_mat, lf_col, precision=lax.Precision.HIGHEST,
                           preferred_element_type=F32)
        bcum_row = lax.dot_general(lf_row, cum_mat, (((1,), (1,)), ((), ())),
                                   precision=lax.Precision.HIGHEST,
                                   preferred_element_type=F32)
        src_row = ig_row - bcum_row
        cmax = ig_col - bcum_col
        k = 1
        while k < chunk:
            cmax = jnp.where(pos_in_chunk >= k, jnp.maximum(cmax, pltpu.roll(cmax, k, axis=0)), cmax)
            k *= 2

        per_chunk = []
        m_prev = m_ref[b]
        for c in range(n_chunks):
            ls = slice(c * chunk, (c + 1) * chunk)
            bc = bcum_col[ls]
            inter = bc + m_prev
            m_t = jnp.maximum(inter, bc + cmax[ls])
            b_last = bc[chunk - 1:chunk, :]
            m_new = m_t[chunk - 1:chunk, :]
            per_chunk.append((
                bc - m_t,
                src_row[:, ls],
                jnp.exp(inter - m_t),
                jnp.exp(-m_t),
                jnp.exp(b_last - bc + ig_col[ls] - m_new),
                jnp.exp(b_last + m_prev - m_new)))
            m_prev = m_new
        m_ref[b] = m_prev
        gates.append(per_chunk)

    for b in range(nb):
        rs = slice(b * ts, (b + 1) * ts)
        for c in range(D_ML // LANES):
            _causal_conv(xem_s, xmc_s, b, c, ts, cwm_ref, cbm_ref, act=jax.nn.silu)
        blocks_per_head = DH_IN // LANES
        for hd in range(H_ML):
            xmc = jnp.concatenate([xmc_s[b, hd * blocks_per_head + i] for i in range(blocks_per_head)],
                                  axis=1).astype(BF16)
            qk = _dot(xmc, _wb(wqk_ref, hd))
            qk_s[rs, hd * 2 * DK:hd * 2 * DK + DK] = (qk[:, :DK] * (DK ** -0.5)).astype(BF16)
            qk_s[rs, hd * 2 * DK + DK:(hd + 1) * 2 * DK] = qk[:, DK:].astype(BF16)

        for c in range(n_chunks):
            gs = slice(b * ts + c * chunk, b * ts + (c + 1) * chunk)
            col_t, src_rows, w_inter, floor, wk_col, decay = gates[b][c]
            for hd in range(H_ML):
                q = qk_s[gs, hd * 2 * DK:hd * 2 * DK + DK]
                kk = qk_s[gs, hd * 2 * DK + DK:(hd + 1) * 2 * DK]
                v = v_s[gs, hd * DV:(hd + 1) * DV]
                c_old = c_ref[b, hd]
                n_old = n_ref[b, hd:hd + 1, :]
                dmat = col_t[:, hd:hd + 1] + src_rows[hd:hd + 1, :]
                w_intra = jnp.exp(jnp.where(tri, dmat, -jnp.inf))
                s = _dot_nt(q, kk) * w_intra
                wi = w_inter[:, hd:hd + 1]
                num = _dot(s.astype(BF16), v) + wi * _dot_nt(q, c_old.astype(BF16))
                qn = jnp.sum(q.astype(F32) * n_old.astype(BF16).astype(F32), axis=1, keepdims=True)
                den = jnp.sum(s, axis=1, keepdims=True) + wi * qn
                hm_s[gs, hd * DV:(hd + 1) * DV] = num / jnp.maximum(jnp.abs(den), floor[:, hd:hd + 1])
                kw = kk.astype(F32) * wk_col[:, hd:hd + 1]
                dec = decay[:, hd:hd + 1]
                c_ref[b, hd] = dec * c_old + _dot_tn(v, kw.astype(BF16))
                n_ref[b, hd:hd + 1, :] = dec * n_old + jnp.sum(kw, axis=0, keepdims=True)

    xrgr = _dot(u, _wb(wxrgr_ref))
    gr_s[...] = xrgr[:, D_RNN:]
    for b in range(nb):
        for c in range(RG_BLOCKS):
            xer_s[b, c, SUBLANES:, :] = xrgr[b * ts:(b + 1) * ts, lane_blk(c)]
    rate = -RG_C * jax.nn.softplus(-lam_ref[...])
    for b in range(nb):
        rs = slice(b * ts, (b + 1) * ts)
        for c in range(RG_BLOCKS):
            cs = lane_blk(c)
            _causal_conv(xer_s, xc_s, b, c, ts, cwr_ref, cbr_ref)
            g = _dot(xc_s[b, c].astype(BF16), _wb(wax_ref, c))
            gate_s[b, c, 0] = g[:, :RG_BW] + ba_ref[:, cs]
            gate_s[b, c, 1] = g[:, RG_BW:] + bx_ref[:, cs]
            h_ref[b, :, cs] = _rglru_scan(xc_s, gate_s, b, c, ts, rate[:, cs], h_ref[b, :, cs])
            ya_s[rs, cs] = (xc_s[b, c] * jax.nn.gelu(gr_s[rs, cs])).astype(BF16)

    y_b = (jax.nn.sigmoid(_dot(u, _wb(wzm_ref))) * hm_s[...]).astype(BF16)
    gab = _dot(u, _wb(wgab_ref))
    mix = (jax.nn.sigmoid(gab[:, :D_MODEL]) * _dot(ya_s[...], _wb(wpa_ref))
           + jax.nn.sigmoid(gab[:, D_MODEL:]) * _dot(y_b, _wb(wpb_ref)))
    y_ref[...] = x_ref[...] + _dot(mix.astype(BF16), _wb(wout_ref)).reshape(nb, ts, D_MODEL)

    @pl.when(j == nj - 1)
    def _():
        for c in range(D_RNN // LANES):
            convr_ref[:, :, lane_blk(c)] = xer_s[:, c, tail, :]
        for c in range(D_ML // LANES):
            convm_ref[:, :, lane_blk(c)] = xem_s[:, c, tail, :]

    xer_s[:, :, hist, :] = xer_s[:, :, tail, :]
    xem_s[:, :, hist, :] = xem_s[:, :, tail, :]


def _mlp_kernel(x_ref, gmlp_ref, wup_ref, wdown_ref, gfin_ref, y_ref):
    x = x_ref[...]
    hid = jnp.maximum(_dot(_rmsnorm(x, gmlp_ref[...]).astype(BF16), _wb(wup_ref)), 0.0)
    x = x + _dot((hid * hid).astype(BF16), _wb(wdown_ref))
    y_ref[...] = _rmsnorm(x, gfin_ref[...])


def _resident(shape):
    zeros = (0,) * len(shape)
    return pl.BlockSpec(shape, lambda *_: zeros, pipeline_mode=pl.Buffered(1))


def _mixer(x, conv_r, h, conv_m, c, n, m, weights, *, nb, ts, chunk):
    batch, seq, _ = x.shape
    assert batch % nb == 0 and seq % ts == 0 and ts % chunk == 0 and chunk % SUBLANES == 0
    assert ts % SPAN == 0
    grid = (batch // nb, seq // ts)
    rows = nb * ts
    h = h.reshape(batch, 1, D_RNN)
    m = m.reshape(batch, 1, H_ML)

    def per_batch(shape, **kw):
        nd = len(shape)
        return pl.BlockSpec((nb,) + shape[1:], lambda i, j: (i,) + (0,) * (nd - 1), **kw)

    state_in = (conv_r, h, conv_m, c, n, m)
    in_specs = ([pl.BlockSpec((nb, ts, D_MODEL), lambda i, j: (i, j, 0))]
                + [per_batch(s.shape, pipeline_mode=pl.Buffered(1)) for s in state_in]
                + [_resident(w.shape) for w in weights])
    out_shape = ([jax.ShapeDtypeStruct(x.shape, F32)]
                 + [jax.ShapeDtypeStruct(s.shape, F32) for s in state_in])
    out_specs = ([pl.BlockSpec((nb, ts, D_MODEL), lambda i, j: (i, j, 0))]
                 + [per_batch(s.shape, pipeline_mode=pl.Buffered(1)) for s in state_in])
    scratch = [
        pltpu.VMEM((nb, D_RNN // LANES, SUBLANES + ts, LANES), F32),
        pltpu.VMEM((nb, D_ML // LANES, SUBLANES + ts, LANES), F32),
        pltpu.VMEM((nb, D_RNN // LANES, ts, LANES), F32),
        pltpu.VMEM((nb, RG_BLOCKS, 2, ts, LANES), F32),
        pltpu.VMEM((nb, D_ML // LANES, ts, LANES), F32),
        pltpu.VMEM((rows, D_RNN), F32),
        pltpu.VMEM((rows, D_ML), BF16),
        pltpu.VMEM((rows, D_RNN), BF16),
        pltpu.VMEM((rows, H_ML * 2 * DK), BF16),
        pltpu.VMEM((rows, H_ML * DV), BF16),
        pltpu.VMEM((rows, H_ML * DV), F32),
    ]
    outs = pl.pallas_call(
        functools.partial(_mixer_kernel, nb=nb, ts=ts, chunk=chunk),
        grid=grid,
        in_specs=in_specs,
        out_specs=out_specs,
        out_shape=out_shape,
        scratch_shapes=scratch,
        compiler_params=pltpu.CompilerParams(
            dimension_semantics=("arbitrary", "arbitrary"),
            vmem_limit_bytes=VMEM_LIMIT_BYTES),
    )(x, *state_in, *weights)
    y, conv_r, h, conv_m, c, n, m = outs
    return y, conv_r, h.reshape(batch, D_RNN), conv_m, c, n, m.reshape(batch, H_ML)


def _mlp(x, weights, *, rows):
    batch, seq, _ = x.shape
    total = batch * seq
    assert total % rows == 0
    x2 = x.reshape(total, D_MODEL)
    y = pl.pallas_call(
        _mlp_kernel,
        grid=(total // rows,),
        in_specs=[pl.BlockSpec((rows, D_MODEL), lambda i: (i, 0))]
                 + [_resident(w.shape) for w in weights],
        out_specs=pl.BlockSpec((rows, D_MODEL), lambda i: (i, 0)),
        out_shape=jax.ShapeDtypeStruct((total, D_MODEL), F32),
        compiler_params=pltpu.CompilerParams(
            dimension_semantics=("arbitrary",),
            vmem_limit_bytes=VMEM_LIMIT_BYTES),
    )(x2, *weights)
    return y.reshape(batch, seq, D_MODEL)


def kernel(x_prompt, x_sample, state_rglru_conv, state_rglru_h, state_mlstm_conv, state_mlstm_C, state_mlstm_n, state_mlstm_m, norm_mix, w_in, b_if, conv_rglru_w, conv_rglru_b, rglru_wa, rglru_ba, rglru_wx, rglru_bx, rglru_lambda, conv_mlstm_w, conv_mlstm_b, mlstm_wq, mlstm_wk, mlstm_wv, w_branch_a, w_branch_b, w_out, norm_mlp, w_mlp_up, w_mlp_down, norm_final):
    depth = w_in.shape[0]
    assert depth == 1, "the final norm is fused into the (single) layer's MLP kernel"
    l = 0
    bp = x_prompt.shape[0]
    row = lambda t: t.reshape(1, -1)
    wl = w_in[l]
    o_xm = 2 * D_RNN
    o_zm = o_xm + D_ML
    o_i = o_zm + D_ML
    o_f = o_i + H_ML
    o_g = o_f + H_ML
    assert o_xm == D_ML, "the first three column groups of w_in are packed as equal-width blocks"
    lane_pad = lambda t: jnp.pad(t, ((0, 0), (0, LANES - t.shape[1])))
    mixer_w = (
        row(norm_mix[l]),
        _pack_rows(wl, 0, D_ML),
        _pack_rows(wl, 1, D_ML),
        _pack_rows(wl, 2, D_ML),
        _pack_rows(wl[:, o_g:]),
        jnp.concatenate([lane_pad(wl[:, o_i:o_f]), lane_pad(wl[:, o_f:o_g])], axis=1).astype(BF16),
        jnp.concatenate([lane_pad(row(b_if[l, :H_ML])), lane_pad(row(b_if[l, H_ML:]))], axis=1),
        conv_rglru_w[l], row(conv_rglru_b[l]),
        _pack_rows(jnp.concatenate([rglru_wa[l], rglru_wx[l]], axis=-1)),
        row(rglru_ba[l]), row(rglru_bx[l]), row(rglru_lambda[l]),
        conv_mlstm_w[l], row(conv_mlstm_b[l]),
        _pack_rows(jnp.concatenate([mlstm_wq[l], mlstm_wk[l]], axis=-1)),
        _pack_rows(mlstm_wv[l]),
        _pack_rows(w_branch_a[l]), _pack_rows(w_branch_b[l]), _pack_rows(w_out[l]),
    )
    outp = _mixer(x_prompt,
                  jnp.zeros((bp, CONV_W - 1, D_RNN), F32), jnp.zeros((bp, D_RNN), F32),
                  jnp.zeros((bp, CONV_W - 1, D_ML), F32), jnp.zeros((bp, H_ML, DV, DK), F32),
                  jnp.zeros((bp, H_ML, DK), F32), jnp.zeros((bp, H_ML), F32),
                  mixer_w, nb=1, ts=256, chunk=256)
    outs = _mixer(x_sample, state_rglru_conv[l], state_rglru_h[l], state_mlstm_conv[l],
                  state_mlstm_C[l], state_mlstm_n[l], state_mlstm_m[l],
                  mixer_w, nb=4, ts=64, chunk=64)
    mlp_w = (row(norm_mlp[l]), _pack_rows(w_mlp_up[l]), _pack_rows(w_mlp_down[l]), row(norm_final))
    y_prompt = _mlp(outp[0], mlp_w, rows=512)
    y_sample = _mlp(outs[0], mlp_w, rows=512)
    p_state = [t[None] for t in outp[1:]]
    s_state = [t[None] for t in outs[1:]]
    return (y_prompt, y_sample, *p_state, *s_state)
```

```python
import functools

import jax
import jax.numpy as jnp
from jax import lax
from jax.experimental import pallas as pl
from jax.experimental.pallas import tpu as pltpu

D_MODEL = 1024
CONV_W = 4
D_RNN = 1024
RG_BLOCKS = 8
RG_BW = D_RNN // RG_BLOCKS
RG_C = 8.0
D_ML = 2 * D_MODEL
H_ML = 8
DH_IN = D_ML // H_ML
DK = 128
DV = 256
D_FF = 4 * D_MODEL
EPS = 1e-6

LANES = 128
SUBLANES = 8
ROW_STRIDE = 4
SPAN = SUBLANES * ROW_STRIDE
VMEM_LIMIT_BYTES = 60 * 1024 * 1024
PACK_ROWS = 256

F32 = jnp.float32
BF16 = jnp.bfloat16


def _dot(a, b):
    return jnp.dot(a, b, preferred_element_type=F32)


def _dot_nt(a, b):
    return lax.dot_general(a, b, (((1,), (1,)), ((), ())), preferred_element_type=F32)


def _dot_tn(a, b):
    return lax.dot_general(a, b, (((0,), (0,)), ((), ())), preferred_element_type=F32)


def _wb(w_ref, *idx):
    return pltpu.bitcast(w_ref[idx] if idx else w_ref[...], BF16)


def _pack_kernel(w_ref, o_ref):
    o_ref[...] = pltpu.bitcast(w_ref[...].astype(BF16), jnp.uint32)


def _pack_rows(w, col_block=0, n=None):
    *lead, k, n_all = w.shape
    n = n_all if n is None else n
    w2 = w.reshape(-1, n_all)
    total = w2.shape[0]
    rb = min(total, PACK_ROWS)
    assert k % 2 == 0 and total % rb == 0 and n % LANES == 0
    packed = pl.pallas_call(
        _pack_kernel,
        grid=(total // rb,),
        in_specs=[pl.BlockSpec((rb, n), lambda i: (i, col_block))],
        out_specs=pl.BlockSpec((rb // 2, n), lambda i: (i, 0)),
        out_shape=jax.ShapeDtypeStruct((total // 2, n), jnp.uint32),
    )(w2)
    return packed.reshape(*lead, k // 2, n)


def _rmsnorm(x, g):
    return x * lax.rsqrt(jnp.mean(x * x, axis=-1, keepdims=True) + EPS) * g


def _strided(start):
    return pl.ds(start, SUBLANES, stride=ROW_STRIDE)


def _causal_conv(xe_ref, out_ref, b, c, t, w_ref, b_ref, act=None):
    cs = slice(c * LANES, (c + 1) * LANES)
    taps = [jnp.broadcast_to(w_ref[k:k + 1, cs], (SUBLANES, LANES)) for k in range(CONV_W)]
    bias = jnp.broadcast_to(b_ref[:, cs], (SUBLANES, LANES))
    for start in range(0, t, SPAN):
        for r in range(ROW_STRIDE):
            y = bias
            for k in range(CONV_W):
                y = y + xe_ref[b, c, _strided(SUBLANES + start + r - k), :] * taps[CONV_W - 1 - k]
            out_ref[b, c, _strided(start + r), :] = y if act is None else act(y)


def _rglru_scan(xc_ref, gate_ref, b, c, t, rate, h0):
    row = lax.broadcasted_iota(jnp.int32, (SUBLANES, LANES), 0)
    rate = jnp.broadcast_to(rate, (SUBLANES, LANES))
    carry = jnp.broadcast_to(h0, (SUBLANES, LANES))
    for start in range(0, t, SPAN):
        h_loc, a_cum = [], []
        for r in range(ROW_STRIDE):
            rows = _strided(start + r)
            xc = xc_ref[b, c, rows, :]
            log_a = jax.nn.sigmoid(gate_ref[b, c, 0, rows, :]) * rate
            a = jnp.exp(log_a)
            gain = jnp.sqrt(-jnp.tanh(log_a) * (a * a + 1.0))
            bt = gain * (jax.nn.sigmoid(gate_ref[b, c, 1, rows, :]) * xc)
            if r == 0:
                h_loc.append(bt)
                a_cum.append(a)
            else:
                h_loc.append(a * h_loc[-1] + bt)
                a_cum.append(a * a_cum[-1])
        h_end, a_end = h_loc[-1], a_cum[-1]
        k = 1
        while k < SUBLANES:
            keep = row >= k
            h_end = h_end + a_end * jnp.where(keep, pltpu.roll(h_end, k, axis=0), 0.0)
            a_end = a_end * jnp.where(keep, pltpu.roll(a_end, k, axis=0), 1.0)
            k *= 2
        h_end = h_end + a_end * carry
        h_in = jnp.where(row >= 1, pltpu.roll(h_end, 1, axis=0), carry)
        for r in range(ROW_STRIDE):
            xc_ref[b, c, _strided(start + r), :] = h_loc[r] + a_cum[r] * h_in
        carry = jnp.broadcast_to(h_end[SUBLANES - 1:SUBLANES, :], (SUBLANES, LANES))
    return carry[0:1, :]


def _mixer_kernel(
        x_ref, convr0_ref, h0_ref, convm0_ref, c0_ref, n0_ref, m0_ref,
        gmix_ref, wxrgr_ref, wxm_ref, wzm_ref, wgab_ref,
        wif_ref, bif_ref,
        cwr_ref, cbr_ref, wax_ref, ba_ref, bx_ref, lam_ref,
        cwm_ref, cbm_ref, wqk_ref, wv_ref, wpa_ref, wpb_ref, wout_ref,
        y_ref, convr_ref, h_ref, convm_ref, c_ref, n_ref, m_ref,
        xer_s, xem_s, xc_s, gate_s, xmc_s, gr_s, xmb_s, ya_s, qk_s, v_s, hm_s,
        *, nb, ts, chunk, single_tile):
    j = pl.program_id(1)
    nj = pl.num_programs(1)
    rows = nb * ts
    hist = slice(SUBLANES - (CONV_W - 1), SUBLANES)
    tail = slice(SUBLANES + ts - (CONV_W - 1), SUBLANES + ts)
    lane_blk = lambda c: slice(c * LANES, (c + 1) * LANES)

    @pl.when(j == 0)
    def _():
        for c in range(D_RNN // LANES):
            xer_s[:, c, hist, :] = convr0_ref[:, :, lane_blk(c)]
        for c in range(D_ML // LANES):
            xem_s[:, c, hist, :] = convm0_ref[:, :, lane_blk(c)]
        if not single_tile:
            h_ref[...] = h0_ref[...]
            c_ref[...] = c0_ref[...]
            n_ref[...] = n0_ref[...]
            m_ref[...] = m0_ref[...]

    h_in, m_in = (h0_ref, m0_ref) if single_tile else (h_ref, m_ref)

    u = _rmsnorm(x_ref[...].reshape(rows, D_MODEL), gmix_ref[...]).astype(BF16)

    n_chunks = ts // chunk
    tpos = lax.broadcasted_iota(jnp.int32, (ts, ts), 0)
    spos = lax.broadcasted_iota(jnp.int32, (ts, ts), 1)
    if n_chunks == 1:
        same_chunk_tri = spos <= tpos
    else:
        same_chunk_tri = (spos <= tpos) & ((tpos // chunk) == (spos // chunk))
    cum_mat = same_chunk_tri.astype(F32)
    tri = (lax.broadcasted_iota(jnp.int32, (chunk, chunk), 1)
           <= lax.broadcasted_iota(jnp.int32, (chunk, chunk), 0))
    pos_in_chunk = lax.broadcasted_iota(jnp.int32, (ts, H_ML), 0) % chunk

    g_if_all = _dot(u, wif_ref[...]) + bif_ref[...]

    xm = _dot(u, _wb(wxm_ref))
    xmb_s[...] = xm.astype(BF16)
    for b in range(nb):
        for c in range(D_ML // LANES):
            xem_s[b, c, SUBLANES:, :] = xm[b * ts:(b + 1) * ts, lane_blk(c)]
    for hd in range(H_ML):
        cs = slice(hd * DH_IN, (hd + 1) * DH_IN)
        v_s[:, cs] = _dot(xmb_s[:, cs], _wb(wv_ref, hd)).astype(BF16)

    gates = []
    for b in range(nb):
        g_if = g_if_all[b * ts:(b + 1) * ts]
        ig_col = g_if[:, :H_ML]
        lf_col = jax.nn.log_sigmoid(g_if[:, LANES:LANES + H_ML])
        ig_row = g_if[:, :LANES].T[:H_ML]
        lf_row = jax.nn.log_sigmoid(g_if[:, LANES:].T[:H_ML])
        bcum_col = jnp.dot(cum_mat, lf_col, precision=lax.Precision.HIGHEST,
                           preferred_element_type=F32)
        bcum_row = lax.dot_general(lf_row, cum_mat, (((1,), (1,)), ((), ())),
                                   precision=lax.Precision.HIGHEST,
                                   preferred_element_type=F32)
        src_row = ig_row - bcum_row
        cmax = ig_col - bcum_col
        k = 1
        while k < chunk:
            cmax = jnp.where(pos_in_chunk >= k, jnp.maximum(cmax, pltpu.roll(cmax, k, axis=0)), cmax)
            k *= 2

        per_chunk = []
        m_prev = m_in[b]
        for c in range(n_chunks):
            ls = slice(c * chunk, (c + 1) * chunk)
            bc = bcum_col[ls]
            inter = bc + m_prev
            m_t = jnp.maximum(inter, bc + cmax[ls])
            b_last = bc[chunk - 1:chunk, :]
            m_new = m_t[chunk - 1:chunk, :]
            per_chunk.append((
                bc - m_t,
                src_row[:, ls],
                jnp.exp(inter - m_t),
                jnp.exp(-m_t),
                jnp.exp(b_last - bc + ig_col[ls] - m_new),
                jnp.exp(b_last + m_prev - m_new)))
            m_prev = m_new
        m_ref[b] = m_prev
        gates.append(per_chunk)

    for b in range(nb):
        rs = slice(b * ts, (b + 1) * ts)
        for c in range(D_ML // LANES):
            _causal_conv(xem_s, xmc_s, b, c, ts, cwm_ref, cbm_ref, act=jax.nn.silu)
        blocks_per_head = DH_IN // LANES
        for hd in range(H_ML):
            xmc = jnp.concatenate([xmc_s[b, hd * blocks_per_head + i] for i in range(blocks_per_head)],
                                  axis=1).astype(BF16)
            qk = _dot(xmc, _wb(wqk_ref, hd))
            qk_s[rs, hd * 2 * DK:hd * 2 * DK + DK] = (qk[:, :DK] * (DK ** -0.5)).astype(BF16)
            qk_s[rs, hd * 2 * DK + DK:(hd + 1) * 2 * DK] = qk[:, DK:].astype(BF16)

        for c in range(n_chunks):
            gs = slice(b * ts + c * chunk, b * ts + (c + 1) * chunk)
            col_t, src_rows, w_inter, floor, wk_col, decay = gates[b][c]
            for hd in range(H_ML):
                q = qk_s[gs, hd * 2 * DK:hd * 2 * DK + DK]
                kk = qk_s[gs, hd * 2 * DK + DK:(hd + 1) * 2 * DK]
                v = v_s[gs, hd * DV:(hd + 1) * DV]
                c_in, n_in = (c0_ref, n0_ref) if single_tile and c == 0 else (c_ref, n_ref)
                c_old = c_in[b, hd]
                n_old = n_in[b, hd:hd + 1, :]
                dmat = col_t[:, hd:hd + 1] + src_rows[hd:hd + 1, :]
                w_intra = jnp.exp(jnp.where(tri, dmat, -jnp.inf))
                s = _dot_nt(q, kk) * w_intra
                wi = w_inter[:, hd:hd + 1]
                num = _dot(s.astype(BF16), v) + wi * _dot_nt(q, c_old.astype(BF16))
                qn = jnp.sum(q.astype(F32) * n_old.astype(BF16).astype(F32), axis=1, keepdims=True)
                den = jnp.sum(s, axis=1, keepdims=True) + wi * qn
                hm_s[gs, hd * DV:(hd + 1) * DV] = num / jnp.maximum(jnp.abs(den), floor[:, hd:hd + 1])
                kw = kk.astype(F32) * wk_col[:, hd:hd + 1]
                dec = decay[:, hd:hd + 1]
                c_ref[b, hd] = dec * c_old + _dot_tn(v, kw.astype(BF16))
                n_ref[b, hd:hd + 1, :] = dec * n_old + jnp.sum(kw, axis=0, keepdims=True)

    xrgr = _dot(u, _wb(wxrgr_ref))
    gr_s[...] = xrgr[:, D_RNN:]
    for b in range(nb):
        for c in range(RG_BLOCKS):
            xer_s[b, c, SUBLANES:, :] = xrgr[b * ts:(b + 1) * ts, lane_blk(c)]
    rate = -RG_C * jax.nn.softplus(-lam_ref[...])
    for b in range(nb):
        rs = slice(b * ts, (b + 1) * ts)
        for c in range(RG_BLOCKS):
            cs = lane_blk(c)
            _causal_conv(xer_s, xc_s, b, c, ts, cwr_ref, cbr_ref)
            g = _dot(xc_s[b, c].astype(BF16), _wb(wax_ref, c))
            gate_s[b, c, 0] = g[:, :RG_BW] + ba_ref[:, cs]
            gate_s[b, c, 1] = g[:, RG_BW:] + bx_ref[:, cs]
            h_ref[b, :, cs] = _rglru_scan(xc_s, gate_s, b, c, ts, rate[:, cs], h_in[b, :, cs])
            ya_s[rs, cs] = (xc_s[b, c] * jax.nn.gelu(gr_s[rs, cs])).astype(BF16)

    y_b = (jax.nn.sigmoid(_dot(u, _wb(wzm_ref))) * hm_s[...]).astype(BF16)
    gab = _dot(u, _wb(wgab_ref))
    mix = (jax.nn.sigmoid(gab[:, :D_MODEL]) * _dot(ya_s[...], _wb(wpa_ref))
           + jax.nn.sigmoid(gab[:, D_MODEL:]) * _dot(y_b, _wb(wpb_ref)))
    y_ref[...] = x_ref[...] + _dot(mix.astype(BF16), _wb(wout_ref)).reshape(nb, ts, D_MODEL)

    @pl.when(j == nj - 1)
    def _():
        for c in range(D_RNN // LANES):
            convr_ref[:, :, lane_blk(c)] = xer_s[:, c, tail, :]
        for c in range(D_ML // LANES):
            convm_ref[:, :, lane_blk(c)] = xem_s[:, c, tail, :]

    xer_s[:, :, hist, :] = xer_s[:, :, tail, :]
    xem_s[:, :, hist, :] = xem_s[:, :, tail, :]


def _mlp_kernel(x_ref, gmlp_ref, wup_ref, wdown_ref, gfin_ref, y_ref):
    x = x_ref[...]
    hid = jnp.maximum(_dot(_rmsnorm(x, gmlp_ref[...]).astype(BF16), _wb(wup_ref)), 0.0)
    x = x + _dot((hid * hid).astype(BF16), _wb(wdown_ref))
    y_ref[...] = _rmsnorm(x, gfin_ref[...])


def _resident(shape):
    zeros = (0,) * len(shape)
    return pl.BlockSpec(shape, lambda *_: zeros, pipeline_mode=pl.Buffered(1))


def _mixer(x, conv_r, h, conv_m, c, n, m, weights, *, nb, ts, chunk):
    batch, seq, _ = x.shape
    assert batch % nb == 0 and seq % ts == 0 and ts % chunk == 0 and chunk % SUBLANES == 0
    assert ts % SPAN == 0
    grid = (batch // nb, seq // ts)
    rows = nb * ts
    h = h.reshape(batch, 1, D_RNN)
    m = m.reshape(batch, 1, H_ML)

    def per_batch(shape, **kw):
        nd = len(shape)
        return pl.BlockSpec((nb,) + shape[1:], lambda i, j: (i,) + (0,) * (nd - 1), **kw)

    state_in = (conv_r, h, conv_m, c, n, m)
    in_specs = ([pl.BlockSpec((nb, ts, D_MODEL), lambda i, j: (i, j, 0))]
                + [per_batch(s.shape, pipeline_mode=pl.Buffered(1)) for s in state_in]
                + [_resident(w.shape) for w in weights])
    out_shape = ([jax.ShapeDtypeStruct(x.shape, F32)]
                 + [jax.ShapeDtypeStruct(s.shape, F32) for s in state_in])
    out_specs = ([pl.BlockSpec((nb, ts, D_MODEL), lambda i, j: (i, j, 0))]
                 + [per_batch(s.shape, pipeline_mode=pl.Buffered(1)) for s in state_in])
    scratch = [
        pltpu.VMEM((nb, D_RNN // LANES, SUBLANES + ts, LANES), F32),
        pltpu.VMEM((nb, D_ML // LANES, SUBLANES + ts, LANES), F32),
        pltpu.VMEM((nb, D_RNN // LANES, ts, LANES), F32),
        pltpu.VMEM((nb, RG_BLOCKS, 2, ts, LANES), F32),
        pltpu.VMEM((nb, D_ML // LANES, ts, LANES), F32),
        pltpu.VMEM((rows, D_RNN), F32),
        pltpu.VMEM((rows, D_ML), BF16),
        pltpu.VMEM((rows, D_RNN), BF16),
        pltpu.VMEM((rows, H_ML * 2 * DK), BF16),
        pltpu.VMEM((rows, H_ML * DV), BF16),
        pltpu.VMEM((rows, H_ML * DV), F32),
    ]
    outs = pl.pallas_call(
        functools.partial(_mixer_kernel, nb=nb, ts=ts, chunk=chunk, single_tile=seq == ts),
        grid=grid,
        in_specs=in_specs,
        out_specs=out_specs,
        out_shape=out_shape,
        scratch_shapes=scratch,
        compiler_params=pltpu.CompilerParams(
            dimension_semantics=("arbitrary", "arbitrary"),
            vmem_limit_bytes=VMEM_LIMIT_BYTES),
    )(x, *state_in, *weights)
    y, conv_r, h, conv_m, c, n, m = outs
    return y, conv_r, h.reshape(batch, D_RNN), conv_m, c, n, m.reshape(batch, H_ML)


def _mlp(x, weights, *, rows):
    batch, seq, _ = x.shape
    total = batch * seq
    assert total % rows == 0
    x2 = x.reshape(total, D_MODEL)
    y = pl.pallas_call(
        _mlp_kernel,
        grid=(total // rows,),
        in_specs=[pl.BlockSpec((rows, D_MODEL), lambda i: (i, 0))]
                 + [_resident(w.shape) for w in weights],
        out_specs=pl.BlockSpec((rows, D_MODEL), lambda i: (i, 0)),
        out_shape=jax.ShapeDtypeStruct((total, D_MODEL), F32),
        compiler_params=pltpu.CompilerParams(
            dimension_semantics=("arbitrary",),
            vmem_limit_bytes=VMEM_LIMIT_BYTES),
    )(x2, *weights)
    return y.reshape(batch, seq, D_MODEL)


def kernel(x_prompt, x_sample, state_rglru_conv, state_rglru_h, state_mlstm_conv, state_mlstm_C, state_mlstm_n, state_mlstm_m, norm_mix, w_in, b_if, conv_rglru_w, conv_rglru_b, rglru_wa, rglru_ba, rglru_wx, rglru_bx, rglru_lambda, conv_mlstm_w, conv_mlstm_b, mlstm_wq, mlstm_wk, mlstm_wv, w_branch_a, w_branch_b, w_out, norm_mlp, w_mlp_up, w_mlp_down, norm_final):
    depth = w_in.shape[0]
    assert depth == 1, "the final norm is fused into the (single) layer's MLP kernel"
    l = 0
    bp = x_prompt.shape[0]
    row = lambda t: t.reshape(1, -1)
    wl = w_in[l]
    o_xm = 2 * D_RNN
    o_zm = o_xm + D_ML
    o_i = o_zm + D_ML
    o_f = o_i + H_ML
    o_g = o_f + H_ML
    assert o_xm == D_ML, "the first three column groups of w_in are packed as equal-width blocks"
    lane_pad = lambda t: jnp.pad(t, ((0, 0), (0, LANES - t.shape[1])))
    mixer_w = (
        row(norm_mix[l]),
        _pack_rows(wl, 0, D_ML),
        _pack_rows(wl, 1, D_ML),
        _pack_rows(wl, 2, D_ML),
        _pack_rows(wl[:, o_g:]),
        jnp.concatenate([lane_pad(wl[:, o_i:o_f].astype(BF16)), lane_pad(wl[:, o_f:o_g].astype(BF16))], axis=1),
        jnp.concatenate([lane_pad(row(b_if[l, :H_ML])), lane_pad(row(b_if[l, H_ML:]))], axis=1),
        conv_rglru_w[l], row(conv_rglru_b[l]),
        _pack_rows(jnp.concatenate([rglru_wa[l], rglru_wx[l]], axis=-1)),
        row(rglru_ba[l]), row(rglru_bx[l]), row(rglru_lambda[l]),
        conv_mlstm_w[l], row(conv_mlstm_b[l]),
        _pack_rows(jnp.concatenate([mlstm_wq[l], mlstm_wk[l]], axis=-1)),
        _pack_rows(mlstm_wv[l]),
        _pack_rows(w_branch_a[l]), _pack_rows(w_branch_b[l]), _pack_rows(w_out[l]),
    )
    outp = _mixer(x_prompt,
                  jnp.zeros((bp, CONV_W - 1, D_RNN), F32), jnp.zeros((bp, D_RNN), F32),
                  jnp.zeros((bp, CONV_W - 1, D_ML), F32), jnp.zeros((bp, H_ML, DV, DK), F32),
                  jnp.zeros((bp, H_ML, DK), F32), jnp.zeros((bp, H_ML), F32),
                  mixer_w, nb=1, ts=256, chunk=256)
    outs = _mixer(x_sample, state_rglru_conv[l], state_rglru_h[l], state_mlstm_conv[l],
                  state_mlstm_C[l], state_mlstm_n[l], state_mlstm_m[l],
                  mixer_w, nb=4, ts=64, chunk=64)
    mlp_w = (row(norm_mlp[l]), _pack_rows(w_mlp_up[l]), _pack_rows(w_mlp_down[l]), row(norm_final))
    y_prompt = _mlp(outp[0], mlp_w, rows=512)
    y_sample = _mlp(outs[0], mlp_w, rows=512)
    p_state = [t[None] for t in outp[1:]]
    s_state = [t[None] for t in outs[1:]]
    return (y_prompt, y_sample, *p_state, *s_state)
```

```python
import functools

import jax
import jax.numpy as jnp
from jax import lax
from jax.experimental import pallas as pl
from jax.experimental.pallas import tpu as pltpu

D_MODEL = 1024
CONV_W = 4
D_RNN = 1024
RG_BLOCKS = 8
RG_BW = D_RNN // RG_BLOCKS
RG_C = 8.0
D_ML = 2 * D_MODEL
H_ML = 8
DH_IN = D_ML // H_ML
DK = 128
DV = 256
D_FF = 4 * D_MODEL
EPS = 1e-6

LANES = 128
SUBLANES = 8
ROW_STRIDE = 4
SPAN = SUBLANES * ROW_STRIDE
VMEM_LIMIT_BYTES = 60 * 1024 * 1024
PACK_ROWS = 256

F32 = jnp.float32
BF16 = jnp.bfloat16


def _dot(a, b):
    return jnp.dot(a, b, preferred_element_type=F32)


def _dot_nt(a, b):
    return lax.dot_general(a, b, (((1,), (1,)), ((), ())), preferred_element_type=F32)


def _dot_tn(a, b):
    return lax.dot_general(a, b, (((0,), (0,)), ((), ())), preferred_element_type=F32)


def _wb(w_ref, *idx):
    return pltpu.bitcast(w_ref[idx] if idx else w_ref[...], BF16)


def _pack_kernel(w_ref, o_ref):
    o_ref[...] = pltpu.bitcast(w_ref[...].astype(BF16), jnp.uint32)


def _pack_rows(w, col_block=0, n=None):
    *lead, k, n_all = w.shape
    n = n_all if n is None else n
    w2 = w.reshape(-1, n_all)
    total = w2.shape[0]
    rb = min(total, PACK_ROWS)
    assert k % 2 == 0 and total % rb == 0 and n % LANES == 0
    packed = pl.pallas_call(
        _pack_kernel,
        grid=(total // rb,),
        in_specs=[pl.BlockSpec((rb, n), lambda i: (i, col_block))],
        out_specs=pl.BlockSpec((rb // 2, n), lambda i: (i, 0)),
        out_shape=jax.ShapeDtypeStruct((total // 2, n), jnp.uint32),
    )(w2)
    return packed.reshape(*lead, k // 2, n)


def _pack_t_kernel(wt_ref, o_ref):
    o_ref[...] = pltpu.bitcast(wt_ref[...].T.astype(BF16), jnp.uint32)


def _pack_rows_t(wt, row_block, n):
    _, k = wt.shape
    rb = min(k, PACK_ROWS)
    assert k % rb == 0 and rb % 2 == 0 and n % LANES == 0
    return pl.pallas_call(
        _pack_t_kernel,
        grid=(k // rb,),
        in_specs=[pl.BlockSpec((n, rb), lambda i: (row_block, i))],
        out_specs=pl.BlockSpec((rb // 2, n), lambda i: (i, 0)),
        out_shape=jax.ShapeDtypeStruct((k // 2, n), jnp.uint32),
    )(wt)


def _gate_w_kernel(wt_ref, o_ref):
    k = wt_ref.shape[1]
    w_t = jnp.concatenate([wt_ref[...], jnp.zeros((LANES - 2 * H_ML, k), F32)], axis=0).T
    lane = lax.broadcasted_iota(jnp.int32, w_t.shape, 1)
    o_ref[:, :LANES] = jnp.where(lane < H_ML, w_t, 0.0).astype(BF16)
    o_ref[:, LANES:] = jnp.where(lane < H_ML, pltpu.roll(w_t, LANES - H_ML, axis=1), 0.0).astype(BF16)


def _gate_weights(wt, row):
    _, k = wt.shape
    assert row % (2 * H_ML) == 0
    return pl.pallas_call(
        _gate_w_kernel,
        grid=(1,),
        in_specs=[pl.BlockSpec((2 * H_ML, k), lambda i: (row // (2 * H_ML), 0))],
        out_specs=pl.BlockSpec((k, 2 * LANES), lambda i: (0, 0)),
        out_shape=jax.ShapeDtypeStruct((k, 2 * LANES), BF16),
    )(wt)


def _rmsnorm(x, g):
    return x * lax.rsqrt(jnp.mean(x * x, axis=-1, keepdims=True) + EPS) * g


def _strided(start):
    return pl.ds(start, SUBLANES, stride=ROW_STRIDE)


def _causal_conv(xe_ref, out_ref, b, c, t, w_ref, b_ref, act=None):
    cs = slice(c * LANES, (c + 1) * LANES)
    taps = [jnp.broadcast_to(w_ref[k:k + 1, cs], (SUBLANES, LANES)) for k in range(CONV_W)]
    bias = jnp.broadcast_to(b_ref[:, cs], (SUBLANES, LANES))
    for start in range(0, t, SPAN):
        for r in range(ROW_STRIDE):
            y = bias
            for k in range(CONV_W):
                y = y + xe_ref[b, c, _strided(SUBLANES + start + r - k), :] * taps[CONV_W - 1 - k]
            out_ref[b, c, _strided(start + r), :] = y if act is None else act(y)


def _rglru_scan(xc_ref, gate_ref, b, c, t, rate, h0):
    row = lax.broadcasted_iota(jnp.int32, (SUBLANES, LANES), 0)
    rate = jnp.broadcast_to(rate, (SUBLANES, LANES))
    carry = jnp.broadcast_to(h0, (SUBLANES, LANES))
    for start in range(0, t, SPAN):
        h_loc, a_cum = [], []
        for r in range(ROW_STRIDE):
            rows = _strided(start + r)
            xc = xc_ref[b, c, rows, :]
            log_a = jax.nn.sigmoid(gate_ref[b, c, 0, rows, :]) * rate
            a = jnp.exp(log_a)
            gain = jnp.sqrt(-jnp.tanh(log_a) * (a * a + 1.0))
            bt = gain * (jax.nn.sigmoid(gate_ref[b, c, 1, rows, :]) * xc)
            if r == 0:
                h_loc.append(bt)
                a_cum.append(a)
            else:
                h_loc.append(a * h_loc[-1] + bt)
                a_cum.append(a * a_cum[-1])
        h_end, a_end = h_loc[-1], a_cum[-1]
        k = 1
        while k < SUBLANES:
            keep = row >= k
            h_end = h_end + a_end * jnp.where(keep, pltpu.roll(h_end, k, axis=0), 0.0)
            a_end = a_end * jnp.where(keep, pltpu.roll(a_end, k, axis=0), 1.0)
            k *= 2
        h_end = h_end + a_end * carry
        h_in = jnp.where(row >= 1, pltpu.roll(h_end, 1, axis=0), carry)
        for r in range(ROW_STRIDE):
            xc_ref[b, c, _strided(start + r), :] = h_loc[r] + a_cum[r] * h_in
        carry = jnp.broadcast_to(h_end[SUBLANES - 1:SUBLANES, :], (SUBLANES, LANES))
    return carry[0:1, :]


def _mixer_kernel(
        x_ref, convr0_ref, h0_ref, convm0_ref, c0_ref, n0_ref, m0_ref,
        gmix_ref, wxrgr_ref, wxm_ref, wzm_ref, wgab_ref,
        wif_ref, bif_ref,
        cwr_ref, cbr_ref, wax_ref, ba_ref, bx_ref, lam_ref,
        cwm_ref, cbm_ref, wqk_ref, wv_ref, wpa_ref, wpb_ref, wout_ref,
        y_ref, convr_ref, h_ref, convm_ref, c_ref, n_ref, m_ref,
        xer_s, xem_s, xc_s, gate_s, xmc_s, gr_s, xmb_s, ya_s, qk_s, v_s, hm_s,
        *, nb, ts, chunk, single_tile):
    j = pl.program_id(1)
    nj = pl.num_programs(1)
    rows = nb * ts
    hist = slice(SUBLANES - (CONV_W - 1), SUBLANES)
    tail = slice(SUBLANES + ts - (CONV_W - 1), SUBLANES + ts)
    lane_blk = lambda c: slice(c * LANES, (c + 1) * LANES)

    @pl.when(j == 0)
    def _():
        for c in range(D_RNN // LANES):
            xer_s[:, c, hist, :] = convr0_ref[:, :, lane_blk(c)]
        for c in range(D_ML // LANES):
            xem_s[:, c, hist, :] = convm0_ref[:, :, lane_blk(c)]
        if not single_tile:
            h_ref[...] = h0_ref[...]
            c_ref[...] = c0_ref[...]
            n_ref[...] = n0_ref[...]
            m_ref[...] = m0_ref[...]

    h_in, m_in = (h0_ref, m0_ref) if single_tile else (h_ref, m_ref)

    u = _rmsnorm(x_ref[...].reshape(rows, D_MODEL), gmix_ref[...]).astype(BF16)

    n_chunks = ts // chunk
    tpos = lax.broadcasted_iota(jnp.int32, (ts, ts), 0)
    spos = lax.broadcasted_iota(jnp.int32, (ts, ts), 1)
    if n_chunks == 1:
        same_chunk_tri = spos <= tpos
    else:
        same_chunk_tri = (spos <= tpos) & ((tpos // chunk) == (spos // chunk))
    cum_mat = same_chunk_tri.astype(F32)
    tri = (lax.broadcasted_iota(jnp.int32, (chunk, chunk), 1)
           <= lax.broadcasted_iota(jnp.int32, (chunk, chunk), 0))
    pos_in_chunk = lax.broadcasted_iota(jnp.int32, (ts, H_ML), 0) % chunk

    g_if_all = _dot(u, wif_ref[...]) + bif_ref[...]

    xm = _dot(u, _wb(wxm_ref))
    xmb_s[...] = xm.astype(BF16)
    for b in range(nb):
        for c in range(D_ML // LANES):
            xem_s[b, c, SUBLANES:, :] = xm[b * ts:(b + 1) * ts, lane_blk(c)]
    for hd in range(H_ML):
        cs = slice(hd * DH_IN, (hd + 1) * DH_IN)
        v_s[:, cs] = _dot(xmb_s[:, cs], _wb(wv_ref, hd)).astype(BF16)

    gates = []
    for b in range(nb):
        g_if = g_if_all[b * ts:(b + 1) * ts]
        ig_col = g_if[:, :H_ML]
        lf_col = jax.nn.log_sigmoid(g_if[:, LANES:LANES + H_ML])
        ig_row = g_if[:, :LANES].T[:H_ML]
        lf_row = jax.nn.log_sigmoid(g_if[:, LANES:].T[:H_ML])
        bcum_col = jnp.dot(cum_mat, lf_col, precision=lax.Precision.HIGHEST,
                           preferred_element_type=F32)
        bcum_row = lax.dot_general(lf_row, cum_mat, (((1,), (1,)), ((), ())),
                                   precision=lax.Precision.HIGHEST,
                                   preferred_element_type=F32)
        src_row = ig_row - bcum_row
        cmax = ig_col - bcum_col
        k = 1
        while k < chunk:
            cmax = jnp.where(pos_in_chunk >= k, jnp.maximum(cmax, pltpu.roll(cmax, k, axis=0)), cmax)
            k *= 2

        per_chunk = []
        m_prev = m_in[b]
        for c in range(n_chunks):
            ls = slice(c * chunk, (c + 1) * chunk)
            bc = bcum_col[ls]
            inter = bc + m_prev
            m_t = jnp.maximum(inter, bc + cmax[ls])
            b_last = bc[chunk - 1:chunk, :]
            m_new = m_t[chunk - 1:chunk, :]
            per_chunk.append((
                bc - m_t,
                src_row[:, ls],
                jnp.exp(inter - m_t),
                jnp.exp(-m_t),
                jnp.exp(b_last - bc + ig_col[ls] - m_new),
                jnp.exp(b_last + m_prev - m_new)))
            m_prev = m_new
        m_ref[b] = m_prev
        gates.append(per_chunk)

    for b in range(nb):
        rs = slice(b * ts, (b + 1) * ts)
        for c in range(D_ML // LANES):
            _causal_conv(xem_s, xmc_s, b, c, ts, cwm_ref, cbm_ref, act=jax.nn.silu)
        blocks_per_head = DH_IN // LANES
        for hd in range(H_ML):
            xmc = jnp.concatenate([xmc_s[b, hd * blocks_per_head + i] for i in range(blocks_per_head)],
                                  axis=1).astype(BF16)
            qk = _dot(xmc, _wb(wqk_ref, hd))
            qk_s[rs, hd * 2 * DK:hd * 2 * DK + DK] = (qk[:, :DK] * (DK ** -0.5)).astype(BF16)
            qk_s[rs, hd * 2 * DK + DK:(hd + 1) * 2 * DK] = qk[:, DK:].astype(BF16)

        for c in range(n_chunks):
            gs = slice(b * ts + c * chunk, b * ts + (c + 1) * chunk)
            col_t, src_rows, w_inter, floor, wk_col, decay = gates[b][c]
            for hd in range(H_ML):
                q = qk_s[gs, hd * 2 * DK:hd * 2 * DK + DK]
                kk = qk_s[gs, hd * 2 * DK + DK:(hd + 1) * 2 * DK]
                v = v_s[gs, hd * DV:(hd + 1) * DV]
                c_in, n_in = (c0_ref, n0_ref) if single_tile and c == 0 else (c_ref, n_ref)
                c_old = c_in[b, hd]
                n_old = n_in[b, hd:hd + 1, :]
                dmat = col_t[:, hd:hd + 1] + src_rows[hd:hd + 1, :]
                w_intra = jnp.exp(jnp.where(tri, dmat, -jnp.inf))
                s = _dot_nt(q, kk) * w_intra
                wi = w_inter[:, hd:hd + 1]
                num = _dot(s.astype(BF16), v) + wi * _dot_nt(q, c_old.astype(BF16))
                qn = jnp.sum(q.astype(F32) * n_old.astype(BF16).astype(F32), axis=1, keepdims=True)
                den = jnp.sum(s, axis=1, keepdims=True) + wi * qn
                hm_s[gs, hd * DV:(hd + 1) * DV] = num / jnp.maximum(jnp.abs(den), floor[:, hd:hd + 1])
                kw = kk.astype(F32) * wk_col[:, hd:hd + 1]
                dec = decay[:, hd:hd + 1]
                c_ref[b, hd] = dec * c_old + _dot_tn(v, kw.astype(BF16))
                n_ref[b, hd:hd + 1, :] = dec * n_old + jnp.sum(kw, axis=0, keepdims=True)

    xrgr = _dot(u, _wb(wxrgr_ref))
    gr_s[...] = xrgr[:, D_RNN:]
    for b in range(nb):
        for c in range(RG_BLOCKS):
            xer_s[b, c, SUBLANES:, :] = xrgr[b * ts:(b + 1) * ts, lane_blk(c)]
    rate = -RG_C * jax.nn.softplus(-lam_ref[...])
    for b in range(nb):
        rs = slice(b * ts, (b + 1) * ts)
        for c in range(RG_BLOCKS):
            cs = lane_blk(c)
            _causal_conv(xer_s, xc_s, b, c, ts, cwr_ref, cbr_ref)
            g = _dot(xc_s[b, c].astype(BF16), _wb(wax_ref, c))
            gate_s[b, c, 0] = g[:, :RG_BW] + ba_ref[:, cs]
            gate_s[b, c, 1] = g[:, RG_BW:] + bx_ref[:, cs]
            h_ref[b, :, cs] = _rglru_scan(xc_s, gate_s, b, c, ts, rate[:, cs], h_in[b, :, cs])
            ya_s[rs, cs] = (xc_s[b, c] * jax.nn.gelu(gr_s[rs, cs])).astype(BF16)

    y_b = (jax.nn.sigmoid(_dot(u, _wb(wzm_ref))) * hm_s[...]).astype(BF16)
    gab = _dot(u, _wb(wgab_ref))
    mix = (jax.nn.sigmoid(gab[:, :D_MODEL]) * _dot(ya_s[...], _wb(wpa_ref))
           + jax.nn.sigmoid(gab[:, D_MODEL:]) * _dot(y_b, _wb(wpb_ref)))
    y_ref[...] = x_ref[...] + _dot(mix.astype(BF16), _wb(wout_ref)).reshape(nb, ts, D_MODEL)

    @pl.when(j == nj - 1)
    def _():
        for c in range(D_RNN // LANES):
            convr_ref[:, :, lane_blk(c)] = xer_s[:, c, tail, :]
        for c in range(D_ML // LANES):
            convm_ref[:, :, lane_blk(c)] = xem_s[:, c, tail, :]

    xer_s[:, :, hist, :] = xer_s[:, :, tail, :]
    xem_s[:, :, hist, :] = xem_s[:, :, tail, :]


def _mlp_kernel(x_ref, gmlp_ref, wup_ref, wdown_ref, gfin_ref, y_ref):
    x = x_ref[...]
    hid = jnp.maximum(_dot(_rmsnorm(x, gmlp_ref[...]).astype(BF16), _wb(wup_ref)), 0.0)
    x = x + _dot((hid * hid).astype(BF16), _wb(wdown_ref))
    y_ref[...] = _rmsnorm(x, gfin_ref[...])


def _resident(shape):
    zeros = (0,) * len(shape)
    return pl.BlockSpec(shape, lambda *_: zeros, pipeline_mode=pl.Buffered(1))


def _mixer(x, conv_r, h, conv_m, c, n, m, weights, *, nb, ts, chunk):
    batch, seq, _ = x.shape
    assert batch % nb == 0 and seq % ts == 0 and ts % chunk == 0 and chunk % SUBLANES == 0
    assert ts % SPAN == 0
    grid = (batch // nb, seq // ts)
    rows = nb * ts
    h = h.reshape(batch, 1, D_RNN)
    m = m.reshape(batch, 1, H_ML)

    def per_batch(shape, **kw):
        nd = len(shape)
        return pl.BlockSpec((nb,) + shape[1:], lambda i, j: (i,) + (0,) * (nd - 1), **kw)

    state_in = (conv_r, h, conv_m, c, n, m)
    in_specs = ([pl.BlockSpec((nb, ts, D_MODEL), lambda i, j: (i, j, 0))]
                + [per_batch(s.shape, pipeline_mode=pl.Buffered(1)) for s in state_in]
                + [_resident(w.shape) for w in weights])
    out_shape = ([jax.ShapeDtypeStruct(x.shape, F32)]
                 + [jax.ShapeDtypeStruct(s.shape, F32) for s in state_in])
    out_specs = ([pl.BlockSpec((nb, ts, D_MODEL), lambda i, j: (i, j, 0))]
                 + [per_batch(s.shape, pipeline_mode=pl.Buffered(1)) for s in state_in])
    scratch = [
        pltpu.VMEM((nb, D_RNN // LANES, SUBLANES + ts, LANES), F32),
        pltpu.VMEM((nb, D_ML // LANES, SUBLANES + ts, LANES), F32),
        pltpu.VMEM((nb, D_RNN // LANES, ts, LANES), F32),
        pltpu.VMEM((nb, RG_BLOCKS, 2, ts, LANES), F32),
        pltpu.VMEM((nb, D_ML // LANES, ts, LANES), F32),
        pltpu.VMEM((rows, D_RNN), F32),
        pltpu.VMEM((rows, D_ML), BF16),
        pltpu.VMEM((rows, D_RNN), BF16),
        pltpu.VMEM((rows, H_ML * 2 * DK), BF16),
        pltpu.VMEM((rows, H_ML * DV), BF16),
        pltpu.VMEM((rows, H_ML * DV), F32),
    ]
    outs = pl.pallas_call(
        functools.partial(_mixer_kernel, nb=nb, ts=ts, chunk=chunk, single_tile=seq == ts),
        grid=grid,
        in_specs=in_specs,
        out_specs=out_specs,
        out_shape=out_shape,
        scratch_shapes=scratch,
        compiler_params=pltpu.CompilerParams(
            dimension_semantics=("arbitrary", "arbitrary"),
            vmem_limit_bytes=VMEM_LIMIT_BYTES),
    )(x, *state_in, *weights)
    y, conv_r, h, conv_m, c, n, m = outs
    return y, conv_r, h.reshape(batch, D_RNN), conv_m, c, n, m.reshape(batch, H_ML)


def _mlp(x, weights, *, rows):
    batch, seq, _ = x.shape
    total = batch * seq
    assert total % rows == 0
    x2 = x.reshape(total, D_MODEL)
    y = pl.pallas_call(
        _mlp_kernel,
        grid=(total // rows,),
        in_specs=[pl.BlockSpec((rows, D_MODEL), lambda i: (i, 0))]
                 + [_resident(w.shape) for w in weights],
        out_specs=pl.BlockSpec((rows, D_MODEL), lambda i: (i, 0)),
        out_shape=jax.ShapeDtypeStruct((total, D_MODEL), F32),
        compiler_params=pltpu.CompilerParams(
            dimension_semantics=("arbitrary",),
            vmem_limit_bytes=VMEM_LIMIT_BYTES),
    )(x2, *weights)
    return y.reshape(batch, seq, D_MODEL)


def kernel(x_prompt, x_sample, state_rglru_conv, state_rglru_h, state_mlstm_conv, state_mlstm_C, state_mlstm_n, state_mlstm_m, norm_mix, w_in, b_if, conv_rglru_w, conv_rglru_b, rglru_wa, rglru_ba, rglru_wx, rglru_bx, rglru_lambda, conv_mlstm_w, conv_mlstm_b, mlstm_wq, mlstm_wk, mlstm_wv, w_branch_a, w_branch_b, w_out, norm_mlp, w_mlp_up, w_mlp_down, norm_final):
    depth = w_in.shape[0]
    assert depth == 1, "the final norm is fused into the (single) layer's MLP kernel"
    l = 0
    bp = x_prompt.shape[0]
    row = lambda t: t.reshape(1, -1)
    wt = jnp.swapaxes(w_in[l], 0, 1)
    o_xm = 2 * D_RNN
    o_zm = o_xm + D_ML
    o_i = o_zm + D_ML
    o_f = o_i + H_ML
    o_g = o_f + H_ML
    assert o_xm == D_ML, "the first three column groups of w_in are packed as equal-width blocks"
    lane_pad = lambda t: jnp.pad(t, ((0, 0), (0, LANES - t.shape[1])))
    mixer_w = (
        row(norm_mix[l]),
        _pack_rows_t(wt, 0, D_ML),
        _pack_rows_t(wt, 1, D_ML),
        _pack_rows_t(wt, 2, D_ML),
        _pack_rows_t(wt[o_g:], 0, 2 * D_MODEL),
        _gate_weights(wt, o_i),
        jnp.concatenate([lane_pad(row(b_if[l, :H_ML])), lane_pad(row(b_if[l, H_ML:]))], axis=1),
        conv_rglru_w[l], row(conv_rglru_b[l]),
        _pack_rows(jnp.concatenate([rglru_wa[l], rglru_wx[l]], axis=-1)),
        row(rglru_ba[l]), row(rglru_bx[l]), row(rglru_lambda[l]),
        conv_mlstm_w[l], row(conv_mlstm_b[l]),
        _pack_rows(jnp.concatenate([mlstm_wq[l], mlstm_wk[l]], axis=-1)),
        _pack_rows(mlstm_wv[l]),
        _pack_rows(w_branch_a[l]), _pack_rows(w_branch_b[l]), _pack_rows(w_out[l]),
    )
    outp = _mixer(x_prompt,
                  jnp.zeros((bp, CONV_W - 1, D_RNN), F32), jnp.zeros((bp, D_RNN), F32),
                  jnp.zeros((bp, CONV_W - 1, D_ML), F32), jnp.zeros((bp, H_ML, DV, DK), F32),
                  jnp.zeros((bp, H_ML, DK), F32), jnp.zeros((bp, H_ML), F32),
                  mixer_w, nb=1, ts=256, chunk=256)
    outs = _mixer(x_sample, state_rglru_conv[l], state_rglru_h[l], state_mlstm_conv[l],
                  state_mlstm_C[l], state_mlstm_n[l], state_mlstm_m[l],
                  mixer_w, nb=4, ts=64, chunk=64)
    mlp_w = (row(norm_mlp[l]), _pack_rows(w_mlp_up[l]), _pack_rows(w_mlp_down[l]), row(norm_final))
    y_prompt = _mlp(outp[0], mlp_w, rows=512)
    y_sample = _mlp(outs[0], mlp_w, rows=512)
    p_state = [t[None] for t in outp[1:]]
    s_state = [t[None] for t in outs[1:]]
    return (y_prompt, y_sample, *p_state, *s_state)
```

```python
import functools

import jax
import jax.numpy as jnp
from jax import lax
from jax.experimental import pallas as pl
from jax.experimental.pallas import tpu as pltpu

D_MODEL = 1024
CONV_W = 4
D_RNN = 1024
RG_BLOCKS = 8
RG_BW = D_RNN // RG_BLOCKS
RG_C = 8.0
D_ML = 2 * D_MODEL
H_ML = 8
DH_IN = D_ML // H_ML
DK = 128
DV = 256
D_FF = 4 * D_MODEL
EPS = 1e-6

LANES = 128
SUBLANES = 8
ROW_STRIDE = 4
SPAN = SUBLANES * ROW_STRIDE
VMEM_LIMIT_BYTES = 60 * 1024 * 1024
PACK_BLOCK_BYTES = 4 * 1024 * 1024

F32 = jnp.float32
BF16 = jnp.bfloat16


def _dot(a, b):
    return jnp.dot(a, b, preferred_element_type=F32)


def _dot_nt(a, b):
    return lax.dot_general(a, b, (((1,), (1,)), ((), ())), preferred_element_type=F32)


def _dot_tn(a, b):
    return lax.dot_general(a, b, (((0,), (0,)), ((), ())), preferred_element_type=F32)


def _wb(w_ref, *idx):
    return pltpu.bitcast(w_ref[idx] if idx else w_ref[...], BF16)


def _pack_kernel(w_ref, o_ref):
    o_ref[...] = pltpu.bitcast(w_ref[...].astype(BF16), jnp.uint32)


def _pack_rows(w, col_block=0, n=None):
    *lead, k, n_all = w.shape
    n = n_all if n is None else n
    w2 = w.reshape(-1, n_all)
    total = w2.shape[0]
    rb = min(total, PACK_BLOCK_BYTES // (4 * n))
    assert k % 2 == 0 and total % rb == 0 and n % LANES == 0
    packed = pl.pallas_call(
        _pack_kernel,
        grid=(total // rb,),
        in_specs=[pl.BlockSpec((rb, n), lambda i: (i, col_block))],
        out_specs=pl.BlockSpec((rb // 2, n), lambda i: (i, 0)),
        out_shape=jax.ShapeDtypeStruct((total // 2, n), jnp.uint32),
    )(w2)
    return packed.reshape(*lead, k // 2, n)


def _pack_t_kernel(wt_ref, o_ref):
    o_ref[...] = pltpu.bitcast(wt_ref[...].T.astype(BF16), jnp.uint32)


def _pack_rows_t(wt, row_block, n):
    _, k = wt.shape
    rb = min(k, PACK_BLOCK_BYTES // (4 * n))
    assert k % rb == 0 and rb % 2 == 0 and n % LANES == 0
    return pl.pallas_call(
        _pack_t_kernel,
        grid=(k // rb,),
        in_specs=[pl.BlockSpec((n, rb), lambda i: (row_block, i))],
        out_specs=pl.BlockSpec((rb // 2, n), lambda i: (i, 0)),
        out_shape=jax.ShapeDtypeStruct((k // 2, n), jnp.uint32),
    )(wt)


def _gate_w_kernel(wt_ref, o_ref):
    k = wt_ref.shape[1]
    w_t = jnp.concatenate([wt_ref[...], jnp.zeros((LANES - 2 * H_ML, k), F32)], axis=0).T
    lane = lax.broadcasted_iota(jnp.int32, w_t.shape, 1)
    o_ref[:, :LANES] = jnp.where(lane < H_ML, w_t, 0.0).astype(BF16)
    o_ref[:, LANES:] = jnp.where(lane < H_ML, pltpu.roll(w_t, LANES - H_ML, axis=1), 0.0).astype(BF16)


def _gate_weights(wt, row):
    _, k = wt.shape
    assert row % (2 * H_ML) == 0
    return pl.pallas_call(
        _gate_w_kernel,
        grid=(1,),
        in_specs=[pl.BlockSpec((2 * H_ML, k), lambda i: (row // (2 * H_ML), 0))],
        out_specs=pl.BlockSpec((k, 2 * LANES), lambda i: (0, 0)),
        out_shape=jax.ShapeDtypeStruct((k, 2 * LANES), BF16),
    )(wt)


def _rmsnorm(x, g):
    return x * lax.rsqrt(jnp.mean(x * x, axis=-1, keepdims=True) + EPS) * g


def _strided(start):
    return pl.ds(start, SUBLANES, stride=ROW_STRIDE)


def _causal_conv(xe_ref, out_ref, b, c, t, w_ref, b_ref, act=None):
    cs = slice(c * LANES, (c + 1) * LANES)
    taps = [jnp.broadcast_to(w_ref[k:k + 1, cs], (SUBLANES, LANES)) for k in range(CONV_W)]
    bias = jnp.broadcast_to(b_ref[:, cs], (SUBLANES, LANES))
    for start in range(0, t, SPAN):
        for r in range(ROW_STRIDE):
            y = bias
            for k in range(CONV_W):
                y = y + xe_ref[b, c, _strided(SUBLANES + start + r - k), :] * taps[CONV_W - 1 - k]
            out_ref[b, c, _strided(start + r), :] = y if act is None else act(y)


def _rglru_scan(xc_ref, gate_ref, b, c, t, rate, h0):
    row = lax.broadcasted_iota(jnp.int32, (SUBLANES, LANES), 0)
    rate = jnp.broadcast_to(rate, (SUBLANES, LANES))
    carry = jnp.broadcast_to(h0, (SUBLANES, LANES))
    for start in range(0, t, SPAN):
        h_loc, a_cum = [], []
        for r in range(ROW_STRIDE):
            rows = _strided(start + r)
            xc = xc_ref[b, c, rows, :]
            log_a = jax.nn.sigmoid(gate_ref[b, c, 0, rows, :]) * rate
            a = jnp.exp(log_a)
            gain = jnp.sqrt(-jnp.tanh(log_a) * (a * a + 1.0))
            bt = gain * (jax.nn.sigmoid(gate_ref[b, c, 1, rows, :]) * xc)
            if r == 0:
                h_loc.append(bt)
                a_cum.append(a)
            else:
                h_loc.append(a * h_loc[-1] + bt)
                a_cum.append(a * a_cum[-1])
        h_end, a_end = h_loc[-1], a_cum[-1]
        k = 1
        while k < SUBLANES:
            keep = row >= k
            h_end = h_end + a_end * jnp.where(keep, pltpu.roll(h_end, k, axis=0), 0.0)
            a_end = a_end * jnp.where(keep, pltpu.roll(a_end, k, axis=0), 1.0)
            k *= 2
        h_end = h_end + a_end * carry
        h_in = jnp.where(row >= 1, pltpu.roll(h_end, 1, axis=0), carry)
        for r in range(ROW_STRIDE):
            xc_ref[b, c, _strided(start + r), :] = h_loc[r] + a_cum[r] * h_in
        carry = jnp.broadcast_to(h_end[SUBLANES - 1:SUBLANES, :], (SUBLANES, LANES))
    return carry[0:1, :]


def _mixer_kernel(
        x_ref, convr0_ref, h0_ref, convm0_ref, c0_ref, n0_ref, m0_ref,
        gmix_ref, wxrgr_ref, wxm_ref, wzm_ref, wgab_ref,
        wif_ref, bif_ref,
        cwr_ref, cbr_ref, wax_ref, ba_ref, bx_ref, lam_ref,
        cwm_ref, cbm_ref, wqk_ref, wv_ref, wpa_ref, wpb_ref, wout_ref,
        y_ref, convr_ref, h_ref, convm_ref, c_ref, n_ref, m_ref,
        xer_s, xem_s, xc_s, gate_s, xmc_s, gr_s, xmb_s, ya_s, qk_s, v_s, hm_s,
        *, nb, ts, chunk, single_tile):
    j = pl.program_id(1)
    nj = pl.num_programs(1)
    rows = nb * ts
    hist = slice(SUBLANES - (CONV_W - 1), SUBLANES)
    tail = slice(SUBLANES + ts - (CONV_W - 1), SUBLANES + ts)
    lane_blk = lambda c: slice(c * LANES, (c + 1) * LANES)

    @pl.when(j == 0)
    def _():
        for c in range(D_RNN // LANES):
            xer_s[:, c, hist, :] = convr0_ref[:, :, lane_blk(c)]
        for c in range(D_ML // LANES):
            xem_s[:, c, hist, :] = convm0_ref[:, :, lane_blk(c)]
        if not single_tile:
            h_ref[...] = h0_ref[...]
            c_ref[...] = c0_ref[...]
            n_ref[...] = n0_ref[...]
            m_ref[...] = m0_ref[...]

    h_in, m_in = (h0_ref, m0_ref) if single_tile else (h_ref, m_ref)

    u = _rmsnorm(x_ref[...].reshape(rows, D_MODEL), gmix_ref[...]).astype(BF16)

    n_chunks = ts // chunk
    tpos = lax.broadcasted_iota(jnp.int32, (ts, ts), 0)
    spos = lax.broadcasted_iota(jnp.int32, (ts, ts), 1)
    if n_chunks == 1:
        same_chunk_tri = spos <= tpos
    else:
        same_chunk_tri = (spos <= tpos) & ((tpos // chunk) == (spos // chunk))
    cum_mat = same_chunk_tri.astype(F32)
    tri = (lax.broadcasted_iota(jnp.int32, (chunk, chunk), 1)
           <= lax.broadcasted_iota(jnp.int32, (chunk, chunk), 0))
    pos_in_chunk = lax.broadcasted_iota(jnp.int32, (ts, H_ML), 0) % chunk

    g_if_all = _dot(u, wif_ref[...]) + bif_ref[...]

    xm = _dot(u, _wb(wxm_ref))
    xmb_s[...] = xm.astype(BF16)
    for b in range(nb):
        for c in range(D_ML // LANES):
            xem_s[b, c, SUBLANES:, :] = xm[b * ts:(b + 1) * ts, lane_blk(c)]
    for hd in range(H_ML):
        cs = slice(hd * DH_IN, (hd + 1) * DH_IN)
        v_s[:, cs] = _dot(xmb_s[:, cs], _wb(wv_ref, hd)).astype(BF16)

    gates = []
    for b in range(nb):
        g_if = g_if_all[b * ts:(b + 1) * ts]
        ig_col = g_if[:, :H_ML]
        lf_col = jax.nn.log_sigmoid(g_if[:, LANES:LANES + H_ML])
        ig_row = g_if[:, :LANES].T[:H_ML]
        lf_row = jax.nn.log_sigmoid(g_if[:, LANES:].T[:H_ML])
        bcum_col = jnp.dot(cum_mat, lf_col, precision=lax.Precision.HIGHEST,
                           preferred_element_type=F32)
        bcum_row = lax.dot_general(lf_row, cum_mat, (((1,), (1,)), ((), ())),
                                   precision=lax.Precision.HIGHEST,
                                   preferred_element_type=F32)
        src_row = ig_row - bcum_row
        cmax = ig_col - bcum_col
        k = 1
        while k < chunk:
            cmax = jnp.where(pos_in_chunk >= k, jnp.maximum(cmax, pltpu.roll(cmax, k, axis=0)), cmax)
            k *= 2

        per_chunk = []
        m_prev = m_in[b]
        for c in range(n_chunks):
            ls = slice(c * chunk, (c + 1) * chunk)
            bc = bcum_col[ls]
            inter = bc + m_prev
            m_t = jnp.maximum(inter, bc + cmax[ls])
            b_last = bc[chunk - 1:chunk, :]
            m_new = m_t[chunk - 1:chunk, :]
            per_chunk.append((
                bc - m_t,
                src_row[:, ls],
                jnp.exp(inter - m_t),
                jnp.exp(-m_t),
                jnp.exp(b_last - bc + ig_col[ls] - m_new),
                jnp.exp(b_last + m_prev - m_new)))
            m_prev = m_new
        m_ref[b] = m_prev
        gates.append(per_chunk)

    for b in range(nb):
        rs = slice(b * ts, (b + 1) * ts)
        for c in range(D_ML // LANES):
            _causal_conv(xem_s, xmc_s, b, c, ts, cwm_ref, cbm_ref, act=jax.nn.silu)
        blocks_per_head = DH_IN // LANES
        for hd in range(H_ML):
            xmc = jnp.concatenate([xmc_s[b, hd * blocks_per_head + i] for i in range(blocks_per_head)],
                                  axis=1).astype(BF16)
            qk = _dot(xmc, _wb(wqk_ref, hd))
            qk_s[rs, hd * 2 * DK:hd * 2 * DK + DK] = (qk[:, :DK] * (DK ** -0.5)).astype(BF16)
            qk_s[rs, hd * 2 * DK + DK:(hd + 1) * 2 * DK] = qk[:, DK:].astype(BF16)

        for c in range(n_chunks):
            gs = slice(b * ts + c * chunk, b * ts + (c + 1) * chunk)
            col_t, src_rows, w_inter, floor, wk_col, decay = gates[b][c]
            for hd in range(H_ML):
                q = qk_s[gs, hd * 2 * DK:hd * 2 * DK + DK]
                kk = qk_s[gs, hd * 2 * DK + DK:(hd + 1) * 2 * DK]
                v = v_s[gs, hd * DV:(hd + 1) * DV]
                c_in, n_in = (c0_ref, n0_ref) if single_tile and c == 0 else (c_ref, n_ref)
                c_old = c_in[b, hd]
                n_old = n_in[b, hd:hd + 1, :]
                dmat = col_t[:, hd:hd + 1] + src_rows[hd:hd + 1, :]
                w_intra = jnp.exp(jnp.where(tri, dmat, -jnp.inf))
                s = _dot_nt(q, kk) * w_intra
                wi = w_inter[:, hd:hd + 1]
                num = _dot(s.astype(BF16), v) + wi * _dot_nt(q, c_old.astype(BF16))
                qn = jnp.sum(q.astype(F32) * n_old.astype(BF16).astype(F32), axis=1, keepdims=True)
                den = jnp.sum(s, axis=1, keepdims=True) + wi * qn
                hm_s[gs, hd * DV:(hd + 1) * DV] = num / jnp.maximum(jnp.abs(den), floor[:, hd:hd + 1])
                kw = kk.astype(F32) * wk_col[:, hd:hd + 1]
                dec = decay[:, hd:hd + 1]
                c_ref[b, hd] = dec * c_old + _dot_tn(v, kw.astype(BF16))
                n_ref[b, hd:hd + 1, :] = dec * n_old + jnp.sum(kw, axis=0, keepdims=True)

    xrgr = _dot(u, _wb(wxrgr_ref))
    gr_s[...] = xrgr[:, D_RNN:]
    for b in range(nb):
        for c in range(RG_BLOCKS):
            xer_s[b, c, SUBLANES:, :] = xrgr[b * ts:(b + 1) * ts, lane_blk(c)]
    rate = -RG_C * jax.nn.softplus(-lam_ref[...])
    for b in range(nb):
        rs = slice(b * ts, (b + 1) * ts)
        for c in range(RG_BLOCKS):
            cs = lane_blk(c)
            _causal_conv(xer_s, xc_s, b, c, ts, cwr_ref, cbr_ref)
            g = _dot(xc_s[b, c].astype(BF16), _wb(wax_ref, c))
            gate_s[b, c, 0] = g[:, :RG_BW] + ba_ref[:, cs]
            gate_s[b, c, 1] = g[:, RG_BW:] + bx_ref[:, cs]
            h_ref[b, :, cs] = _rglru_scan(xc_s, gate_s, b, c, ts, rate[:, cs], h_in[b, :, cs])
            ya_s[rs, cs] = (xc_s[b, c] * jax.nn.gelu(gr_s[rs, cs])).astype(BF16)

    y_b = (jax.nn.sigmoid(_dot(u, _wb(wzm_ref))) * hm_s[...]).astype(BF16)
    gab = _dot(u, _wb(wgab_ref))
    mix = (jax.nn.sigmoid(gab[:, :D_MODEL]) * _dot(ya_s[...], _wb(wpa_ref))
           + jax.nn.sigmoid(gab[:, D_MODEL:]) * _dot(y_b, _wb(wpb_ref)))
    y_ref[...] = x_ref[...] + _dot(mix.astype(BF16), _wb(wout_ref)).reshape(nb, ts, D_MODEL)

    @pl.when(j == nj - 1)
    def _():
        for c in range(D_RNN // LANES):
            convr_ref[:, :, lane_blk(c)] = xer_s[:, c, tail, :]
        for c in range(D_ML // LANES):
            convm_ref[:, :, lane_blk(c)] = xem_s[:, c, tail, :]

    xer_s[:, :, hist, :] = xer_s[:, :, tail, :]
    xem_s[:, :, hist, :] = xem_s[:, :, tail, :]


def _mlp_kernel(x_ref, gmlp_ref, wup_ref, wdown_ref, gfin_ref, y_ref):
    x = x_ref[...]
    hid = jnp.maximum(_dot(_rmsnorm(x, gmlp_ref[...]).astype(BF16), _wb(wup_ref)), 0.0)
    x = x + _dot((hid * hid).astype(BF16), _wb(wdown_ref))
    y_ref[...] = _rmsnorm(x, gfin_ref[...])


def _resident(shape):
    zeros = (0,) * len(shape)
    return pl.BlockSpec(shape, lambda *_: zeros, pipeline_mode=pl.Buffered(1))


def _mixer(x, conv_r, h, conv_m, c, n, m, weights, *, nb, ts, chunk):
    batch, seq, _ = x.shape
    assert batch % nb == 0 and seq % ts == 0 and ts % chunk == 0 and chunk % SUBLANES == 0
    assert ts % SPAN == 0
    grid = (batch // nb, seq // ts)
    rows = nb * ts
    h = h.reshape(batch, 1, D_RNN)
    m = m.reshape(batch, 1, H_ML)

    def per_batch(shape, **kw):
        nd = len(shape)
        return pl.BlockSpec((nb,) + shape[1:], lambda i, j: (i,) + (0,) * (nd - 1), **kw)

    state_in = (conv_r, h, conv_m, c, n, m)
    in_specs = ([pl.BlockSpec((nb, ts, D_MODEL), lambda i, j: (i, j, 0))]
                + [per_batch(s.shape, pipeline_mode=pl.Buffered(1)) for s in state_in]
                + [_resident(w.shape) for w in weights])
    out_shape = ([jax.ShapeDtypeStruct(x.shape, F32)]
                 + [jax.ShapeDtypeStruct(s.shape, F32) for s in state_in])
    out_specs = ([pl.BlockSpec((nb, ts, D_MODEL), lambda i, j: (i, j, 0))]
                 + [per_batch(s.shape, pipeline_mode=pl.Buffered(1)) for s in state_in])
    scratch = [
        pltpu.VMEM((nb, D_RNN // LANES, SUBLANES + ts, LANES), F32),
        pltpu.VMEM((nb, D_ML // LANES, SUBLANES + ts, LANES), F32),
        pltpu.VMEM((nb, D_RNN // LANES, ts, LANES), F32),
        pltpu.VMEM((nb, RG_BLOCKS, 2, ts, LANES), F32),
        pltpu.VMEM((nb, D_ML // LANES, ts, LANES), F32),
        pltpu.VMEM((rows, D_RNN), F32),
        pltpu.VMEM((rows, D_ML), BF16),
        pltpu.VMEM((rows, D_RNN), BF16),
        pltpu.VMEM((rows, H_ML * 2 * DK), BF16),
        pltpu.VMEM((rows, H_ML * DV), BF16),
        pltpu.VMEM((rows, H_ML * DV), F32),
    ]
    outs = pl.pallas_call(
        functools.partial(_mixer_kernel, nb=nb, ts=ts, chunk=chunk, single_tile=seq == ts),
        grid=grid,
        in_specs=in_specs,
        out_specs=out_specs,
        out_shape=out_shape,
        scratch_shapes=scratch,
        compiler_params=pltpu.CompilerParams(
            dimension_semantics=("arbitrary", "arbitrary"),
            vmem_limit_bytes=VMEM_LIMIT_BYTES),
    )(x, *state_in, *weights)
    y, conv_r, h, conv_m, c, n, m = outs
    return y, conv_r, h.reshape(batch, D_RNN), conv_m, c, n, m.reshape(batch, H_ML)


def _mlp(x, weights, *, rows):
    batch, seq, _ = x.shape
    total = batch * seq
    assert total % rows == 0
    x2 = x.reshape(total, D_MODEL)
    y = pl.pallas_call(
        _mlp_kernel,
        grid=(total // rows,),
        in_specs=[pl.BlockSpec((rows, D_MODEL), lambda i: (i, 0))]
                 + [_resident(w.shape) for w in weights],
        out_specs=pl.BlockSpec((rows, D_MODEL), lambda i: (i, 0)),
        out_shape=jax.ShapeDtypeStruct((total, D_MODEL), F32),
        compiler_params=pltpu.CompilerParams(
            dimension_semantics=("arbitrary",),
            vmem_limit_bytes=VMEM_LIMIT_BYTES),
    )(x2, *weights)
    return y.reshape(batch, seq, D_MODEL)


def kernel(x_prompt, x_sample, state_rglru_conv, state_rglru_h, state_mlstm_conv, state_mlstm_C, state_mlstm_n, state_mlstm_m, norm_mix, w_in, b_if, conv_rglru_w, conv_rglru_b, rglru_wa, rglru_ba, rglru_wx, rglru_bx, rglru_lambda, conv_mlstm_w, conv_mlstm_b, mlstm_wq, mlstm_wk, mlstm_wv, w_branch_a, w_branch_b, w_out, norm_mlp, w_mlp_up, w_mlp_down, norm_final):
    depth = w_in.shape[0]
    assert depth == 1, "the final norm is fused into the (single) layer's MLP kernel"
    l = 0
    bp = x_prompt.shape[0]
    row = lambda t: t.reshape(1, -1)
    wt = jnp.swapaxes(w_in[l], 0, 1)
    o_xm = 2 * D_RNN
    o_zm = o_xm + D_ML
    o_i = o_zm + D_ML
    o_f = o_i + H_ML
    o_g = o_f + H_ML
    assert o_xm == D_ML, "the first three column groups of w_in are packed as equal-width blocks"
    lane_pad = lambda t: jnp.pad(t, ((0, 0), (0, LANES - t.shape[1])))
    mixer_w = (
        row(norm_mix[l]),
        _pack_rows_t(wt, 0, D_ML),
        _pack_rows_t(wt, 1, D_ML),
        _pack_rows_t(wt, 2, D_ML),
        _pack_rows_t(wt[o_g:], 0, 2 * D_MODEL),
        _gate_weights(wt, o_i),
        jnp.concatenate([lane_pad(row(b_if[l, :H_ML])), lane_pad(row(b_if[l, H_ML:]))], axis=1),
        conv_rglru_w[l], row(conv_rglru_b[l]),
        _pack_rows(jnp.concatenate([rglru_wa[l], rglru_wx[l]], axis=-1)),
        row(rglru_ba[l]), row(rglru_bx[l]), row(rglru_lambda[l]),
        conv_mlstm_w[l], row(conv_mlstm_b[l]),
        _pack_rows(jnp.concatenate([mlstm_wq[l], mlstm_wk[l]], axis=-1)),
        _pack_rows(mlstm_wv[l]),
        _pack_rows(w_branch_a[l]), _pack_rows(w_branch_b[l]), _pack_rows(w_out[l]),
    )
    outp = _mixer(x_prompt,
                  jnp.zeros((bp, CONV_W - 1, D_RNN), F32), jnp.zeros((bp, D_RNN), F32),
                  jnp.zeros((bp, CONV_W - 1, D_ML), F32), jnp.zeros((bp, H_ML, DV, DK), F32),
                  jnp.zeros((bp, H_ML, DK), F32), jnp.zeros((bp, H_ML), F32),
                  mixer_w, nb=1, ts=256, chunk=256)
    outs = _mixer(x_sample, state_rglru_conv[l], state_rglru_h[l], state_mlstm_conv[l],
                  state_mlstm_C[l], state_mlstm_n[l], state_mlstm_m[l],
                  mixer_w, nb=4, ts=64, chunk=64)
    mlp_w = (row(norm_mlp[l]), _pack_rows(w_mlp_up[l]), _pack_rows(w_mlp_down[l]), row(norm_final))
    y_prompt = _mlp(outp[0], mlp_w, rows=512)
    y_sample = _mlp(outs[0], mlp_w, rows=512)
    p_state = [t[None] for t in outp[1:]]
    s_state = [t[None] for t in outs[1:]]
    return (y_prompt, y_sample, *p_state, *s_state)
```

```python
import functools

import jax
import jax.numpy as jnp
from jax import lax
from jax.experimental import pallas as pl
from jax.experimental.pallas import tpu as pltpu

D_MODEL = 1024
CONV_W = 4
D_RNN = 1024
RG_BLOCKS = 8
RG_BW = D_RNN // RG_BLOCKS
RG_C = 8.0
D_ML = 2 * D_MODEL
H_ML = 8
DH_IN = D_ML // H_ML
DK = 128
DV = 256
D_FF = 4 * D_MODEL
EPS = 1e-6

LANES = 128
SUBLANES = 8
ROW_STRIDE = 4
SPAN = SUBLANES * ROW_STRIDE
VMEM_LIMIT_BYTES = 60 * 1024 * 1024
PACK_BLOCK_BYTES = 4 * 1024 * 1024

F32 = jnp.float32
BF16 = jnp.bfloat16


def _dot(a, b):
    return jnp.dot(a, b, preferred_element_type=F32)


def _dot_nt(a, b):
    return lax.dot_general(a, b, (((1,), (1,)), ((), ())), preferred_element_type=F32)


def _dot_tn(a, b):
    return lax.dot_general(a, b, (((0,), (0,)), ((), ())), preferred_element_type=F32)


def _wb(w_ref, *idx):
    return pltpu.bitcast(w_ref[idx] if idx else w_ref[...], BF16)


def _pack_kernel(w_ref, o_ref):
    o_ref[...] = pltpu.bitcast(w_ref[...].astype(BF16), jnp.uint32)


def _pack_rows(w, col_block=0, n=None):
    *lead, k, n_all = w.shape
    n = n_all if n is None else n
    w2 = w.reshape(-1, n_all)
    total = w2.shape[0]
    rb = min(total, PACK_BLOCK_BYTES // (4 * n))
    assert k % 2 == 0 and total % rb == 0 and n % LANES == 0
    packed = pl.pallas_call(
        _pack_kernel,
        grid=(total // rb,),
        in_specs=[pl.BlockSpec((rb, n), lambda i: (i, col_block))],
        out_specs=pl.BlockSpec((rb // 2, n), lambda i: (i, 0)),
        out_shape=jax.ShapeDtypeStruct((total // 2, n), jnp.uint32),
    )(w2)
    return packed.reshape(*lead, k // 2, n)


def _pack_t_kernel(wt_ref, o_ref):
    o_ref[...] = pltpu.bitcast(wt_ref[...].T.astype(BF16), jnp.uint32)


def _pack_rows_t(wt, row_block, n):
    _, k = wt.shape
    rb = min(k, PACK_BLOCK_BYTES // (4 * n))
    assert k % rb == 0 and rb % 2 == 0 and n % LANES == 0
    return pl.pallas_call(
        _pack_t_kernel,
        grid=(k // rb,),
        in_specs=[pl.BlockSpec((n, rb), lambda i: (row_block, i))],
        out_specs=pl.BlockSpec((rb // 2, n), lambda i: (i, 0)),
        out_shape=jax.ShapeDtypeStruct((k // 2, n), jnp.uint32),
    )(wt)


def _gate_w_kernel(wt_ref, o_ref):
    k = wt_ref.shape[1]
    w_t = jnp.concatenate([wt_ref[...], jnp.zeros((LANES - 2 * H_ML, k), F32)], axis=0).T
    lane = lax.broadcasted_iota(jnp.int32, w_t.shape, 1)
    o_ref[:, :LANES] = jnp.where(lane < H_ML, w_t, 0.0).astype(BF16)
    o_ref[:, LANES:] = jnp.where(lane < H_ML, pltpu.roll(w_t, LANES - H_ML, axis=1), 0.0).astype(BF16)


def _gate_weights(wt, row):
    _, k = wt.shape
    assert row % (2 * H_ML) == 0
    return pl.pallas_call(
        _gate_w_kernel,
        grid=(1,),
        in_specs=[pl.BlockSpec((2 * H_ML, k), lambda i: (row // (2 * H_ML), 0))],
        out_specs=pl.BlockSpec((k, 2 * LANES), lambda i: (0, 0)),
        out_shape=jax.ShapeDtypeStruct((k, 2 * LANES), BF16),
    )(wt)


def _rmsnorm(x, g):
    return x * lax.rsqrt(jnp.mean(x * x, axis=-1, keepdims=True) + EPS) * g


def _strided(start):
    return pl.ds(start, SUBLANES, stride=ROW_STRIDE)


def _causal_conv(xe_ref, out_ref, b, c, t, w_ref, b_ref, act=None):
    cs = slice(c * LANES, (c + 1) * LANES)
    taps = [jnp.broadcast_to(w_ref[k:k + 1, cs], (SUBLANES, LANES)) for k in range(CONV_W)]
    bias = jnp.broadcast_to(b_ref[:, cs], (SUBLANES, LANES))
    for start in range(0, t, SPAN):
        for r in range(ROW_STRIDE):
            y = bias
            for k in range(CONV_W):
                y = y + xe_ref[b, c, _strided(SUBLANES + start + r - k), :] * taps[CONV_W - 1 - k]
            out_ref[b, c, _strided(start + r), :] = y if act is None else act(y)


def _rglru_scan(xc_ref, gate_ref, b, c, t, rate, h0):
    row = lax.broadcasted_iota(jnp.int32, (SUBLANES, LANES), 0)
    rate = jnp.broadcast_to(rate, (SUBLANES, LANES))
    carry = jnp.broadcast_to(h0, (SUBLANES, LANES))
    for start in range(0, t, SPAN):
        h_loc, a_cum = [], []
        for r in range(ROW_STRIDE):
            rows = _strided(start + r)
            xc = xc_ref[b, c, rows, :]
            log_a = jax.nn.sigmoid(gate_ref[b, c, 0, rows, :]) * rate
            a = jnp.exp(log_a)
            gain = jnp.sqrt(-jnp.tanh(log_a) * (a * a + 1.0))
            bt = gain * (jax.nn.sigmoid(gate_ref[b, c, 1, rows, :]) * xc)
            if r == 0:
                h_loc.append(bt)
                a_cum.append(a)
            else:
                h_loc.append(a * h_loc[-1] + bt)
                a_cum.append(a * a_cum[-1])
        h_end, a_end = h_loc[-1], a_cum[-1]
        k = 1
        while k < SUBLANES:
            keep = row >= k
            h_end = h_end + a_end * jnp.where(keep, pltpu.roll(h_end, k, axis=0), 0.0)
            a_end = a_end * jnp.where(keep, pltpu.roll(a_end, k, axis=0), 1.0)
            k *= 2
        h_end = h_end + a_end * carry
        h_in = jnp.where(row >= 1, pltpu.roll(h_end, 1, axis=0), carry)
        for r in range(ROW_STRIDE):
            xc_ref[b, c, _strided(start + r), :] = h_loc[r] + a_cum[r] * h_in
        carry = jnp.broadcast_to(h_end[SUBLANES - 1:SUBLANES, :], (SUBLANES, LANES))
    return carry[0:1, :]


def _mixer_kernel(
        x_ref, convr0_ref, h0_ref, convm0_ref, c0_ref, n0_ref, m0_ref,
        gmix_ref, wxrgr_ref, wxm_ref, wzm_ref, wgab_ref,
        wif_ref, bif_ref,
        cwr_ref, cbr_ref, wax_ref, ba_ref, bx_ref, lam_ref,
        cwm_ref, cbm_ref, wqk_ref, wv_ref, wpa_ref, wpb_ref, wout_ref,
        y_ref, convr_ref, h_ref, convm_ref, c_ref, n_ref, m_ref,
        xer_s, xem_s, xc_s, gate_s, xmc_s, gr_s, xmb_s, ya_s, qk_s, v_s, hm_s,
        *, nb, ts, chunk, single_tile):
    j = pl.program_id(1)
    nj = pl.num_programs(1)
    rows = nb * ts
    hist = slice(SUBLANES - (CONV_W - 1), SUBLANES)
    tail = slice(SUBLANES + ts - (CONV_W - 1), SUBLANES + ts)
    lane_blk = lambda c: slice(c * LANES, (c + 1) * LANES)

    @pl.when(j == 0)
    def _():
        for c in range(D_RNN // LANES):
            xer_s[:, c, hist, :] = convr0_ref[:, :, lane_blk(c)]
        for c in range(D_ML // LANES):
            xem_s[:, c, hist, :] = convm0_ref[:, :, lane_blk(c)]
        if not single_tile:
            h_ref[...] = h0_ref[...]
            c_ref[...] = c0_ref[...]
            n_ref[...] = n0_ref[...]
            m_ref[...] = m0_ref[...]

    h_in, m_in = (h0_ref, m0_ref) if single_tile else (h_ref, m_ref)

    u = _rmsnorm(x_ref[...].reshape(rows, D_MODEL), gmix_ref[...]).astype(BF16)

    n_chunks = ts // chunk
    tpos = lax.broadcasted_iota(jnp.int32, (ts, ts), 0)
    spos = lax.broadcasted_iota(jnp.int32, (ts, ts), 1)
    if n_chunks == 1:
        same_chunk_tri = spos <= tpos
    else:
        same_chunk_tri = (spos <= tpos) & ((tpos // chunk) == (spos // chunk))
    cum_mat = same_chunk_tri.astype(F32)
    tri = (lax.broadcasted_iota(jnp.int32, (chunk, chunk), 1)
           <= lax.broadcasted_iota(jnp.int32, (chunk, chunk), 0))
    pos_in_chunk = lax.broadcasted_iota(jnp.int32, (ts, H_ML), 0) % chunk

    g_if_all = _dot(u, wif_ref[...]) + bif_ref[...]

    xm = _dot(u, _wb(wxm_ref))
    xmb_s[...] = xm.astype(BF16)
    for b in range(nb):
        for c in range(D_ML // LANES):
            xem_s[b, c, SUBLANES:, :] = xm[b * ts:(b + 1) * ts, lane_blk(c)]
    for hd in range(H_ML):
        cs = slice(hd * DH_IN, (hd + 1) * DH_IN)
        v_s[:, cs] = _dot(xmb_s[:, cs], _wb(wv_ref, hd)).astype(BF16)

    gates = []
    for b in range(nb):
        g_if = g_if_all[b * ts:(b + 1) * ts]
        ig_col = g_if[:, :H_ML]
        lf_col = jax.nn.log_sigmoid(g_if[:, LANES:LANES + H_ML])
        ig_row = g_if[:, :LANES].T[:H_ML]
        lf_row = jax.nn.log_sigmoid(g_if[:, LANES:].T[:H_ML])
        bcum_col = jnp.dot(cum_mat, lf_col, precision=lax.Precision.HIGHEST,
                           preferred_element_type=F32)
        bcum_row = lax.dot_general(lf_row, cum_mat, (((1,), (1,)), ((), ())),
                                   precision=lax.Precision.HIGHEST,
                                   preferred_element_type=F32)
        src_row = ig_row - bcum_row
        cmax = ig_col - bcum_col
        k = 1
        while k < chunk:
            cmax = jnp.where(pos_in_chunk >= k, jnp.maximum(cmax, pltpu.roll(cmax, k, axis=0)), cmax)
            k *= 2

        per_chunk = []
        m_prev = m_in[b]
        for c in range(n_chunks):
            ls = slice(c * chunk, (c + 1) * chunk)
            bc = bcum_col[ls]
            inter = bc + m_prev
            m_t = jnp.maximum(inter, bc + cmax[ls])
            b_last = bc[chunk - 1:chunk, :]
            m_new = m_t[chunk - 1:chunk, :]
            per_chunk.append((
                bc - m_t,
                src_row[:, ls],
                jnp.exp(inter - m_t),
                jnp.exp(-m_t),
                jnp.exp(b_last - bc + ig_col[ls] - m_new),
                jnp.exp(b_last + m_prev - m_new)))
            m_prev = m_new
        m_ref[b] = m_prev
        gates.append(per_chunk)

    for b in range(nb):
        rs = slice(b * ts, (b + 1) * ts)
        for c in range(D_ML // LANES):
            _causal_conv(xem_s, xmc_s, b, c, ts, cwm_ref, cbm_ref, act=jax.nn.silu)
        blocks_per_head = DH_IN // LANES
        for hd in range(H_ML):
            xmc = jnp.concatenate([xmc_s[b, hd * blocks_per_head + i] for i in range(blocks_per_head)],
                                  axis=1).astype(BF16)
            qk = _dot(xmc, _wb(wqk_ref, hd))
            qk_s[rs, hd * 2 * DK:hd * 2 * DK + DK] = (qk[:, :DK] * (DK ** -0.5)).astype(BF16)
            qk_s[rs, hd * 2 * DK + DK:(hd + 1) * 2 * DK] = qk[:, DK:].astype(BF16)

        for c in range(n_chunks):
            gs = slice(b * ts + c * chunk, b * ts + (c + 1) * chunk)
            col_t, src_rows, w_inter, floor, wk_col, decay = gates[b][c]
            for hd in range(H_ML):
                q = qk_s[gs, hd * 2 * DK:hd * 2 * DK + DK]
                kk = qk_s[gs, hd * 2 * DK + DK:(hd + 1) * 2 * DK]
                v = v_s[gs, hd * DV:(hd + 1) * DV]
                c_in, n_in = (c0_ref, n0_ref) if single_tile and c == 0 else (c_ref, n_ref)
                c_old = c_in[b, hd]
                n_old = n_in[b, hd:hd + 1, :]
                dmat = col_t[:, hd:hd + 1] + src_rows[hd:hd + 1, :]
                w_intra = jnp.exp(jnp.where(tri, dmat, -jnp.inf))
                s = _dot_nt(q, kk) * w_intra
                wi = w_inter[:, hd:hd + 1]
                num = _dot(s.astype(BF16), v) + wi * _dot_nt(q, c_old.astype(BF16))
                qn = jnp.sum(q.astype(F32) * n_old.astype(BF16).astype(F32), axis=1, keepdims=True)
                den = jnp.sum(s, axis=1, keepdims=True) + wi * qn
                hm_s[gs, hd * DV:(hd + 1) * DV] = num / jnp.maximum(jnp.abs(den), floor[:, hd:hd + 1])
                kw = kk.astype(F32) * wk_col[:, hd:hd + 1]
                dec = decay[:, hd:hd + 1]
                c_ref[b, hd] = dec * c_old + _dot_tn(v, kw.astype(BF16))
                n_ref[b, hd:hd + 1, :] = dec * n_old + jnp.sum(kw, axis=0, keepdims=True)

    xrgr = _dot(u, _wb(wxrgr_ref))
    gr_s[...] = xrgr[:, D_RNN:]
    for b in range(nb):
        for c in range(RG_BLOCKS):
            xer_s[b, c, SUBLANES:, :] = xrgr[b * ts:(b + 1) * ts, lane_blk(c)]
    rate = -RG_C * jax.nn.softplus(-lam_ref[...])
    for b in range(nb):
        rs = slice(b * ts, (b + 1) * ts)
        for c in range(RG_BLOCKS):
            cs = lane_blk(c)
            _causal_conv(xer_s, xc_s, b, c, ts, cwr_ref, cbr_ref)
            g = _dot(xc_s[b, c].astype(BF16), _wb(wax_ref, c))
            gate_s[b, c, 0] = g[:, :RG_BW] + ba_ref[:, cs]
            gate_s[b, c, 1] = g[:, RG_BW:] + bx_ref[:, cs]
            h_ref[b, :, cs] = _rglru_scan(xc_s, gate_s, b, c, ts, rate[:, cs], h_in[b, :, cs])
            ya_s[rs, cs] = (xc_s[b, c] * jax.nn.gelu(gr_s[rs, cs])).astype(BF16)

    y_b = (jax.nn.sigmoid(_dot(u, _wb(wzm_ref))) * hm_s[...]).astype(BF16)
    gab = _dot(u, _wb(wgab_ref))
    mix = (jax.nn.sigmoid(gab[:, :D_MODEL]) * _dot(ya_s[...], _wb(wpa_ref))
           + jax.nn.sigmoid(gab[:, D_MODEL:]) * _dot(y_b, _wb(wpb_ref)))
    y_ref[...] = x_ref[...] + _dot(mix.astype(BF16), _wb(wout_ref)).reshape(nb, ts, D_MODEL)

    @pl.when(j == nj - 1)
    def _():
        for c in range(D_RNN // LANES):
            convr_ref[:, :, lane_blk(c)] = xer_s[:, c, tail, :]
        for c in range(D_ML // LANES):
            convm_ref[:, :, lane_blk(c)] = xem_s[:, c, tail, :]

    xer_s[:, :, hist, :] = xer_s[:, :, tail, :]
    xem_s[:, :, hist, :] = xem_s[:, :, tail, :]


def _mlp_kernel(x_ref, gmlp_ref, wup_ref, wdown_ref, gfin_ref, y_ref):
    x = x_ref[...]
    hid = jnp.maximum(_dot(_rmsnorm(x, gmlp_ref[...]).astype(BF16), _wb(wup_ref)), 0.0)
    x = x + _dot((hid * hid).astype(BF16), _wb(wdown_ref))
    y_ref[...] = _rmsnorm(x, gfin_ref[...])


def _resident(shape):
    zeros = (0,) * len(shape)
    return pl.BlockSpec(shape, lambda *_: zeros, pipeline_mode=pl.Buffered(1))


def _mixer(x, conv_r, h, conv_m, c, n, m, weights, *, nb, ts, chunk):
    batch, seq, _ = x.shape
    assert batch % nb == 0 and seq % ts == 0 and ts % chunk == 0 and chunk % SUBLANES == 0
    assert ts % SPAN == 0
    grid = (batch // nb, seq // ts)
    rows = nb * ts
    h = h.reshape(batch, 1, D_RNN)
    m = m.reshape(batch, 1, H_ML)

    def per_batch(shape, **kw):
        nd = len(shape)
        return pl.BlockSpec((nb,) + shape[1:], lambda i, j: (i,) + (0,) * (nd - 1), **kw)

    state_in = (conv_r, h, conv_m, c, n, m)
    in_specs = ([pl.BlockSpec((nb, ts, D_MODEL), lambda i, j: (i, j, 0))]
                + [per_batch(s.shape, pipeline_mode=pl.Buffered(1)) for s in state_in]
                + [_resident(w.shape) for w in weights])
    out_shape = ([jax.ShapeDtypeStruct(x.shape, F32)]
                 + [jax.ShapeDtypeStruct(s.shape, F32) for s in state_in])
    out_specs = ([pl.BlockSpec((nb, ts, D_MODEL), lambda i, j: (i, j, 0))]
                 + [per_batch(s.shape, pipeline_mode=pl.Buffered(1)) for s in state_in])
    scratch = [
        pltpu.VMEM((nb, D_RNN // LANES, SUBLANES + ts, LANES), F32),
        pltpu.VMEM((nb, D_ML // LANES, SUBLANES + ts, LANES), F32),
        pltpu.VMEM((nb, D_RNN // LANES, ts, LANES), F32),
        pltpu.VMEM((nb, RG_BLOCKS, 2, ts, LANES), F32),
        pltpu.VMEM((nb, D_ML // LANES, ts, LANES), F32),
        pltpu.VMEM((rows, D_RNN), F32),
        pltpu.VMEM((rows, D_ML), BF16),
        pltpu.VMEM((rows, D_RNN), BF16),
        pltpu.VMEM((rows, H_ML * 2 * DK), BF16),
        pltpu.VMEM((rows, H_ML * DV), BF16),
        pltpu.VMEM((rows, H_ML * DV), F32),
    ]
    outs = pl.pallas_call(
        functools.partial(_mixer_kernel, nb=nb, ts=ts, chunk=chunk, single_tile=seq == ts),
        grid=grid,
        in_specs=in_specs,
        out_specs=out_specs,
        out_shape=out_shape,
        scratch_shapes=scratch,
        compiler_params=pltpu.CompilerParams(
            dimension_semantics=("arbitrary", "arbitrary"),
            vmem_limit_bytes=VMEM_LIMIT_BYTES),
    )(x, *state_in, *weights)
    y, conv_r, h, conv_m, c, n, m = outs
    return y, conv_r, h.reshape(batch, D_RNN), conv_m, c, n, m.reshape(batch, H_ML)


def _mlp(x, weights, *, rows):
    batch, seq, _ = x.shape
    total = batch * seq
    assert total % rows == 0
    x2 = x.reshape(total, D_MODEL)
    y = pl.pallas_call(
        _mlp_kernel,
        grid=(total // rows,),
        in_specs=[pl.BlockSpec((rows, D_MODEL), lambda i: (i, 0))]
                 + [_resident(w.shape) for w in weights],
        out_specs=pl.BlockSpec((rows, D_MODEL), lambda i: (i, 0)),
        out_shape=jax.ShapeDtypeStruct((total, D_MODEL), F32),
        compiler_params=pltpu.CompilerParams(
            dimension_semantics=("arbitrary",),
            vmem_limit_bytes=VMEM_LIMIT_BYTES),
    )(x2, *weights)
    return y.reshape(batch, seq, D_MODEL)


def kernel(x_prompt, x_sample, state_rglru_conv, state_rglru_h, state_mlstm_conv, state_mlstm_C, state_mlstm_n, state_mlstm_m, norm_mix, w_in, b_if, conv_rglru_w, conv_rglru_b, rglru_wa, rglru_ba, rglru_wx, rglru_bx, rglru_lambda, conv_mlstm_w, conv_mlstm_b, mlstm_wq, mlstm_wk, mlstm_wv, w_branch_a, w_branch_b, w_out, norm_mlp, w_mlp_up, w_mlp_down, norm_final):
    depth = w_in.shape[0]
    assert depth == 1, "the final norm is fused into the (single) layer's MLP kernel"
    l = 0
    bp = x_prompt.shape[0]
    row = lambda t: t.reshape(1, -1)
    wt = jnp.swapaxes(w_in[l], 0, 1)
    o_xm = 2 * D_RNN
    o_zm = o_xm + D_ML
    o_i = o_zm + D_ML
    o_f = o_i + H_ML
    o_g = o_f + H_ML
    assert o_xm == D_ML, "the first three column groups of w_in are packed as equal-width blocks"
    lane_pad = lambda t: jnp.pad(t, ((0, 0), (0, LANES - t.shape[1])))
    mixer_w = (
        row(norm_mix[l]),
        _pack_rows_t(wt, 0, D_ML),
        _pack_rows_t(wt, 1, D_ML),
        _pack_rows_t(wt, 2, D_ML),
        _pack_rows_t(wt[o_g:], 0, 2 * D_MODEL),
        _gate_weights(wt, o_i),
        jnp.concatenate([lane_pad(row(b_if[l, :H_ML])), lane_pad(row(b_if[l, H_ML:]))], axis=1),
        conv_rglru_w[l], row(conv_rglru_b[l]),
        _pack_rows(jnp.concatenate([rglru_wa[l], rglru_wx[l]], axis=-1)),
        row(rglru_ba[l]), row(rglru_bx[l]), row(rglru_lambda[l]),
        conv_mlstm_w[l], row(conv_mlstm_b[l]),
        _pack_rows(jnp.concatenate([mlstm_wq[l], mlstm_wk[l]], axis=-1)),
        _pack_rows(mlstm_wv[l]),
        _pack_rows(w_branch_a[l]), _pack_rows(w_branch_b[l]), _pack_rows(w_out[l]),
    )
    outp = _mixer(x_prompt,
                  jnp.zeros((bp, CONV_W - 1, D_RNN), F32), jnp.zeros((bp, D_RNN), F32),
                  jnp.zeros((bp, CONV_W - 1, D_ML), F32), jnp.zeros((bp, H_ML, DV, DK), F32),
                  jnp.zeros((bp, H_ML, DK), F32), jnp.zeros((bp, H_ML), F32),
                  mixer_w, nb=1, ts=256, chunk=256)
    outs = _mixer(x_sample, state_rglru_conv[l], state_rglru_h[l], state_mlstm_conv[l],
                  state_mlstm_C[l], state_mlstm_n[l], state_mlstm_m[l],
                  mixer_w, nb=4, ts=64, chunk=64)
    mlp_w = (row(norm_mlp[l]), _pack_rows(w_mlp_up[l]), _pack_rows(w_mlp_down[l]), row(norm_final))
    y_prompt = _mlp(outp[0], mlp_w, rows=1024)
    y_sample = _mlp(outs[0], mlp_w, rows=1024)
    p_state = [t[None] for t in outp[1:]]
    s_state = [t[None] for t in outs[1:]]
    return (y_prompt, y_sample, *p_state, *s_state)
```

```python
import functools

import jax
import jax.numpy as jnp
from jax import lax
from jax.experimental import pallas as pl
from jax.experimental.pallas import tpu as pltpu

D_MODEL = 1024
CONV_W = 4
D_RNN = 1024
RG_BLOCKS = 8
RG_BW = D_RNN // RG_BLOCKS
RG_C = 8.0
D_ML = 2 * D_MODEL
H_ML = 8
DH_IN = D_ML // H_ML
DK = 128
DV = 256
D_FF = 4 * D_MODEL
EPS = 1e-6

LANES = 128
SUBLANES = 8
ROW_STRIDE = 4
SPAN = SUBLANES * ROW_STRIDE
VMEM_LIMIT_BYTES = 60 * 1024 * 1024
PACK_BLOCK_BYTES = 8 * 1024 * 1024

F32 = jnp.float32
BF16 = jnp.bfloat16


def _dot(a, b):
    return jnp.dot(a, b, preferred_element_type=F32)


def _dot_nt(a, b):
    return lax.dot_general(a, b, (((1,), (1,)), ((), ())), preferred_element_type=F32)


def _dot_tn(a, b):
    return lax.dot_general(a, b, (((0,), (0,)), ((), ())), preferred_element_type=F32)


def _wb(w_ref, *idx):
    return pltpu.bitcast(w_ref[idx] if idx else w_ref[...], BF16)


def _pack_kernel(w_ref, o_ref):
    o_ref[...] = pltpu.bitcast(w_ref[...].astype(BF16), jnp.uint32)


def _pack_rows(w, col_block=0, n=None):
    *lead, k, n_all = w.shape
    n = n_all if n is None else n
    w2 = w.reshape(-1, n_all)
    total = w2.shape[0]
    rb = min(total, PACK_BLOCK_BYTES // (4 * n))
    assert k % 2 == 0 and total % rb == 0 and n % LANES == 0
    packed = pl.pallas_call(
        _pack_kernel,
        grid=(total // rb,),
        in_specs=[pl.BlockSpec((rb, n), lambda i: (i, col_block))],
        out_specs=pl.BlockSpec((rb // 2, n), lambda i: (i, 0)),
        out_shape=jax.ShapeDtypeStruct((total // 2, n), jnp.uint32),
    )(w2)
    return packed.reshape(*lead, k // 2, n)


def _pack_t_kernel(wt_ref, o_ref):
    o_ref[...] = pltpu.bitcast(wt_ref[...].T.astype(BF16), jnp.uint32)


def _pack_rows_t(wt, row_block, n):
    _, k = wt.shape
    rb = min(k, PACK_BLOCK_BYTES // (4 * n))
    assert k % rb == 0 and rb % 2 == 0 and n % LANES == 0
    return pl.pallas_call(
        _pack_t_kernel,
        grid=(k // rb,),
        in_specs=[pl.BlockSpec((n, rb), lambda i: (row_block, i))],
        out_specs=pl.BlockSpec((rb // 2, n), lambda i: (i, 0)),
        out_shape=jax.ShapeDtypeStruct((k // 2, n), jnp.uint32),
    )(wt)


def _gate_w_kernel(wt_ref, o_ref):
    k = wt_ref.shape[1]
    w_t = jnp.concatenate([wt_ref[...], jnp.zeros((LANES - 2 * H_ML, k), F32)], axis=0).T
    lane = lax.broadcasted_iota(jnp.int32, w_t.shape, 1)
    o_ref[:, :LANES] = jnp.where(lane < H_ML, w_t, 0.0).astype(BF16)
    o_ref[:, LANES:] = jnp.where(lane < H_ML, pltpu.roll(w_t, LANES - H_ML, axis=1), 0.0).astype(BF16)


def _gate_weights(wt, row):
    _, k = wt.shape
    assert row % (2 * H_ML) == 0
    return pl.pallas_call(
        _gate_w_kernel,
        grid=(1,),
        in_specs=[pl.BlockSpec((2 * H_ML, k), lambda i: (row // (2 * H_ML), 0))],
        out_specs=pl.BlockSpec((k, 2 * LANES), lambda i: (0, 0)),
        out_shape=jax.ShapeDtypeStruct((k, 2 * LANES), BF16),
    )(wt)


def _rmsnorm(x, g):
    return x * lax.rsqrt(jnp.mean(x * x, axis=-1, keepdims=True) + EPS) * g


def _strided(start):
    return pl.ds(start, SUBLANES, stride=ROW_STRIDE)


def _causal_conv(xe_ref, out_ref, b, c, t, w_ref, b_ref, act=None):
    cs = slice(c * LANES, (c + 1) * LANES)
    taps = [jnp.broadcast_to(w_ref[k:k + 1, cs], (SUBLANES, LANES)) for k in range(CONV_W)]
    bias = jnp.broadcast_to(b_ref[:, cs], (SUBLANES, LANES))
    for start in range(0, t, SPAN):
        for r in range(ROW_STRIDE):
            y = bias
            for k in range(CONV_W):
                y = y + xe_ref[b, c, _strided(SUBLANES + start + r - k), :] * taps[CONV_W - 1 - k]
            out_ref[b, c, _strided(start + r), :] = y if act is None else act(y)


def _rglru_scan(xc_ref, gate_ref, b, c, t, rate, h0):
    row = lax.broadcasted_iota(jnp.int32, (SUBLANES, LANES), 0)
    rate = jnp.broadcast_to(rate, (SUBLANES, LANES))
    carry = jnp.broadcast_to(h0, (SUBLANES, LANES))
    for start in range(0, t, SPAN):
        h_loc, a_cum = [], []
        for r in range(ROW_STRIDE):
            rows = _strided(start + r)
            xc = xc_ref[b, c, rows, :]
            log_a = jax.nn.sigmoid(gate_ref[b, c, 0, rows, :]) * rate
            a = jnp.exp(log_a)
            gain = jnp.sqrt(-jnp.tanh(log_a) * (a * a + 1.0))
            bt = gain * (jax.nn.sigmoid(gate_ref[b, c, 1, rows, :]) * xc)
            if r == 0:
                h_loc.append(bt)
                a_cum.append(a)
            else:
                h_loc.append(a * h_loc[-1] + bt)
                a_cum.append(a * a_cum[-1])
        h_end, a_end = h_loc[-1], a_cum[-1]
        k = 1
        while k < SUBLANES:
            keep = row >= k
            h_end = h_end + a_end * jnp.where(keep, pltpu.roll(h_end, k, axis=0), 0.0)
            a_end = a_end * jnp.where(keep, pltpu.roll(a_end, k, axis=0), 1.0)
            k *= 2
        h_end = h_end + a_end * carry
        h_in = jnp.where(row >= 1, pltpu.roll(h_end, 1, axis=0), carry)
        for r in range(ROW_STRIDE):
            xc_ref[b, c, _strided(start + r), :] = h_loc[r] + a_cum[r] * h_in
        carry = jnp.broadcast_to(h_end[SUBLANES - 1:SUBLANES, :], (SUBLANES, LANES))
    return carry[0:1, :]


def _mixer_kernel(
        x_ref, convr0_ref, h0_ref, convm0_ref, c0_ref, n0_ref, m0_ref,
        gmix_ref, wxrgr_ref, wxm_ref, wzm_ref, wgab_ref,
        wif_ref, bif_ref,
        cwr_ref, cbr_ref, wax_ref, ba_ref, bx_ref, lam_ref,
        cwm_ref, cbm_ref, wqk_ref, wv_ref, wpa_ref, wpb_ref, wout_ref,
        y_ref, convr_ref, h_ref, convm_ref, c_ref, n_ref, m_ref,
        xer_s, xem_s, xc_s, gate_s, xmc_s, gr_s, xmb_s, ya_s, qk_s, v_s, hm_s,
        *, nb, ts, chunk, single_tile):
    j = pl.program_id(1)
    nj = pl.num_programs(1)
    rows = nb * ts
    hist = slice(SUBLANES - (CONV_W - 1), SUBLANES)
    tail = slice(SUBLANES + ts - (CONV_W - 1), SUBLANES + ts)
    lane_blk = lambda c: slice(c * LANES, (c + 1) * LANES)

    @pl.when(j == 0)
    def _():
        for c in range(D_RNN // LANES):
            xer_s[:, c, hist, :] = convr0_ref[:, :, lane_blk(c)]
        for c in range(D_ML // LANES):
            xem_s[:, c, hist, :] = convm0_ref[:, :, lane_blk(c)]
        if not single_tile:
            h_ref[...] = h0_ref[...]
            c_ref[...] = c0_ref[...]
            n_ref[...] = n0_ref[...]
            m_ref[...] = m0_ref[...]

    h_in, m_in = (h0_ref, m0_ref) if single_tile else (h_ref, m_ref)

    u = _rmsnorm(x_ref[...].reshape(rows, D_MODEL), gmix_ref[...]).astype(BF16)

    n_chunks = ts // chunk
    tpos = lax.broadcasted_iota(jnp.int32, (ts, ts), 0)
    spos = lax.broadcasted_iota(jnp.int32, (ts, ts), 1)
    if n_chunks == 1:
        same_chunk_tri = spos <= tpos
    else:
        same_chunk_tri = (spos <= tpos) & ((tpos // chunk) == (spos // chunk))
    cum_mat = same_chunk_tri.astype(F32)
    tri = (lax.broadcasted_iota(jnp.int32, (chunk, chunk), 1)
           <= lax.broadcasted_iota(jnp.int32, (chunk, chunk), 0))
    pos_in_chunk = lax.broadcasted_iota(jnp.int32, (ts, H_ML), 0) % chunk

    g_if_all = _dot(u, wif_ref[...]) + bif_ref[...]

    xm = _dot(u, _wb(wxm_ref))
    xmb_s[...] = xm.astype(BF16)
    for b in range(nb):
        for c in range(D_ML // LANES):
            xem_s[b, c, SUBLANES:, :] = xm[b * ts:(b + 1) * ts, lane_blk(c)]
    for hd in range(H_ML):
        cs = slice(hd * DH_IN, (hd + 1) * DH_IN)
        v_s[:, cs] = _dot(xmb_s[:, cs], _wb(wv_ref, hd)).astype(BF16)

    gates = []
    for b in range(nb):
        g_if = g_if_all[b * ts:(b + 1) * ts]
        ig_col = g_if[:, :H_ML]
        lf_col = jax.nn.log_sigmoid(g_if[:, LANES:LANES + H_ML])
        ig_row = g_if[:, :LANES].T[:H_ML]
        lf_row = jax.nn.log_sigmoid(g_if[:, LANES:].T[:H_ML])
        bcum_col = jnp.dot(cum_mat, lf_col, precision=lax.Precision.HIGHEST,
                           preferred_element_type=F32)
        bcum_row = lax.dot_general(lf_row, cum_mat, (((1,), (1,)), ((), ())),
                                   precision=lax.Precision.HIGHEST,
                                   preferred_element_type=F32)
        src_row = ig_row - bcum_row
        cmax = ig_col - bcum_col
        k = 1
        while k < chunk:
            cmax = jnp.where(pos_in_chunk >= k, jnp.maximum(cmax, pltpu.roll(cmax, k, axis=0)), cmax)
            k *= 2

        per_chunk = []
        m_prev = m_in[b]
        for c in range(n_chunks):
            ls = slice(c * chunk, (c + 1) * chunk)
            bc = bcum_col[ls]
            inter = bc + m_prev
            m_t = jnp.maximum(inter, bc + cmax[ls])
            b_last = bc[chunk - 1:chunk, :]
            m_new = m_t[chunk - 1:chunk, :]
            per_chunk.append((
                bc - m_t,
                src_row[:, ls],
                jnp.exp(inter - m_t),
                jnp.exp(-m_t),
                jnp.exp(b_last - bc + ig_col[ls] - m_new),
                jnp.exp(b_last + m_prev - m_new)))
            m_prev = m_new
        m_ref[b] = m_prev
        gates.append(per_chunk)

    for b in range(nb):
        rs = slice(b * ts, (b + 1) * ts)
        for c in range(D_ML // LANES):
            _causal_conv(xem_s, xmc_s, b, c, ts, cwm_ref, cbm_ref, act=jax.nn.silu)
        blocks_per_head = DH_IN // LANES
        for hd in range(H_ML):
            xmc = jnp.concatenate([xmc_s[b, hd * blocks_per_head + i] for i in range(blocks_per_head)],
                                  axis=1).astype(BF16)
            qk = _dot(xmc, _wb(wqk_ref, hd))
            qk_s[rs, hd * 2 * DK:hd * 2 * DK + DK] = (qk[:, :DK] * (DK ** -0.5)).astype(BF16)
            qk_s[rs, hd * 2 * DK + DK:(hd + 1) * 2 * DK] = qk[:, DK:].astype(BF16)

        for c in range(n_chunks):
            gs = slice(b * ts + c * chunk, b * ts + (c + 1) * chunk)
            col_t, src_rows, w_inter, floor, wk_col, decay = gates[b][c]
            for hd in range(H_ML):
                q = qk_s[gs, hd * 2 * DK:hd * 2 * DK + DK]
                kk = qk_s[gs, hd * 2 * DK + DK:(hd + 1) * 2 * DK]
                v = v_s[gs, hd * DV:(hd + 1) * DV]
                c_in, n_in = (c0_ref, n0_ref) if single_tile and c == 0 else (c_ref, n_ref)
                c_old = c_in[b, hd]
                n_old = n_in[b, hd:hd + 1, :]
                dmat = col_t[:, hd:hd + 1] + src_rows[hd:hd + 1, :]
                w_intra = jnp.exp(jnp.where(tri, dmat, -jnp.inf))
                s = _dot_nt(q, kk) * w_intra
                wi = w_inter[:, hd:hd + 1]
                num = _dot(s.astype(BF16), v) + wi * _dot_nt(q, c_old.astype(BF16))
                qn = jnp.sum(q.astype(F32) * n_old.astype(BF16).astype(F32), axis=1, keepdims=True)
                den = jnp.sum(s, axis=1, keepdims=True) + wi * qn
                hm_s[gs, hd * DV:(hd + 1) * DV] = num / jnp.maximum(jnp.abs(den), floor[:, hd:hd + 1])
                kw = kk.astype(F32) * wk_col[:, hd:hd + 1]
                dec = decay[:, hd:hd + 1]
                c_ref[b, hd] = dec * c_old + _dot_tn(v, kw.astype(BF16))
                n_ref[b, hd:hd + 1, :] = dec * n_old + jnp.sum(kw, axis=0, keepdims=True)

    xrgr = _dot(u, _wb(wxrgr_ref))
    gr_s[...] = xrgr[:, D_RNN:]
    for b in range(nb):
        for c in range(RG_BLOCKS):
            xer_s[b, c, SUBLANES:, :] = xrgr[b * ts:(b + 1) * ts, lane_blk(c)]
    rate = -RG_C * jax.nn.softplus(-lam_ref[...])
    for b in range(nb):
        rs = slice(b * ts, (b + 1) * ts)
        for c in range(RG_BLOCKS):
            cs = lane_blk(c)
            _causal_conv(xer_s, xc_s, b, c, ts, cwr_ref, cbr_ref)
            g = _dot(xc_s[b, c].astype(BF16), _wb(wax_ref, c))
            gate_s[b, c, 0] = g[:, :RG_BW] + ba_ref[:, cs]
            gate_s[b, c, 1] = g[:, RG_BW:] + bx_ref[:, cs]
            h_ref[b, :, cs] = _rglru_scan(xc_s, gate_s, b, c, ts, rate[:, cs], h_in[b, :, cs])
            ya_s[rs, cs] = (xc_s[b, c] * jax.nn.gelu(gr_s[rs, cs])).astype(BF16)

    y_b = (jax.nn.sigmoid(_dot(u, _wb(wzm_ref))) * hm_s[...]).astype(BF16)
    gab = _dot(u, _wb(wgab_ref))
    mix = (jax.nn.sigmoid(gab[:, :D_MODEL]) * _dot(ya_s[...], _wb(wpa_ref))
           + jax.nn.sigmoid(gab[:, D_MODEL:]) * _dot(y_b, _wb(wpb_ref)))
    y_ref[...] = x_ref[...] + _dot(mix.astype(BF16), _wb(wout_ref)).reshape(nb, ts, D_MODEL)

    @pl.when(j == nj - 1)
    def _():
        for c in range(D_RNN // LANES):
            convr_ref[:, :, lane_blk(c)] = xer_s[:, c, tail, :]
        for c in range(D_ML // LANES):
            convm_ref[:, :, lane_blk(c)] = xem_s[:, c, tail, :]

    xer_s[:, :, hist, :] = xer_s[:, :, tail, :]
    xem_s[:, :, hist, :] = xem_s[:, :, tail, :]


def _mlp_kernel(x_ref, gmlp_ref, wup_ref, wdown_ref, gfin_ref, y_ref):
    x = x_ref[...]
    hid = jnp.maximum(_dot(_rmsnorm(x, gmlp_ref[...]).astype(BF16), _wb(wup_ref)), 0.0)
    x = x + _dot((hid * hid).astype(BF16), _wb(wdown_ref))
    y_ref[...] = _rmsnorm(x, gfin_ref[...])


def _resident(shape):
    zeros = (0,) * len(shape)
    return pl.BlockSpec(shape, lambda *_: zeros, pipeline_mode=pl.Buffered(1))


def _mixer(x, conv_r, h, conv_m, c, n, m, weights, *, nb, ts, chunk):
    batch, seq, _ = x.shape
    assert batch % nb == 0 and seq % ts == 0 and ts % chunk == 0 and chunk % SUBLANES == 0
    assert ts % SPAN == 0
    grid = (batch // nb, seq // ts)
    rows = nb * ts
    h = h.reshape(batch, 1, D_RNN)
    m = m.reshape(batch, 1, H_ML)

    def per_batch(shape, **kw):
        nd = len(shape)
        return pl.BlockSpec((nb,) + shape[1:], lambda i, j: (i,) + (0,) * (nd - 1), **kw)

    state_in = (conv_r, h, conv_m, c, n, m)
    in_specs = ([pl.BlockSpec((nb, ts, D_MODEL), lambda i, j: (i, j, 0))]
                + [per_batch(s.shape, pipeline_mode=pl.Buffered(1)) for s in state_in]
                + [_resident(w.shape) for w in weights])
    out_shape = ([jax.ShapeDtypeStruct(x.shape, F32)]
                 + [jax.ShapeDtypeStruct(s.shape, F32) for s in state_in])
    out_specs = ([pl.BlockSpec((nb, ts, D_MODEL), lambda i, j: (i, j, 0))]
                 + [per_batch(s.shape, pipeline_mode=pl.Buffered(1)) for s in state_in])
    scratch = [
        pltpu.VMEM((nb, D_RNN // LANES, SUBLANES + ts, LANES), F32),
        pltpu.VMEM((nb, D_ML // LANES, SUBLANES + ts, LANES), F32),
        pltpu.VMEM((nb, D_RNN // LANES, ts, LANES), F32),
        pltpu.VMEM((nb, RG_BLOCKS, 2, ts, LANES), F32),
        pltpu.VMEM((nb, D_ML // LANES, ts, LANES), F32),
        pltpu.VMEM((rows, D_RNN), F32),
        pltpu.VMEM((rows, D_ML), BF16),
        pltpu.VMEM((rows, D_RNN), BF16),
        pltpu.VMEM((rows, H_ML * 2 * DK), BF16),
        pltpu.VMEM((rows, H_ML * DV), BF16),
        pltpu.VMEM((rows, H_ML * DV), F32),
    ]
    outs = pl.pallas_call(
        functools.partial(_mixer_kernel, nb=nb, ts=ts, chunk=chunk, single_tile=seq == ts),
        grid=grid,
        in_specs=in_specs,
        out_specs=out_specs,
        out_shape=out_shape,
        scratch_shapes=scratch,
        compiler_params=pltpu.CompilerParams(
            dimension_semantics=("arbitrary", "arbitrary"),
            vmem_limit_bytes=VMEM_LIMIT_BYTES),
    )(x, *state_in, *weights)
    y, conv_r, h, conv_m, c, n, m = outs
    return y, conv_r, h.reshape(batch, D_RNN), conv_m, c, n, m.reshape(batch, H_ML)


def _mlp(x, weights, *, rows):
    batch, seq, _ = x.shape
    total = batch * seq
    assert total % rows == 0
    x2 = x.reshape(total, D_MODEL)
    y = pl.pallas_call(
        _mlp_kernel,
        grid=(total // rows,),
        in_specs=[pl.BlockSpec((rows, D_MODEL), lambda i: (i, 0))]
                 + [_resident(w.shape) for w in weights],
        out_specs=pl.BlockSpec((rows, D_MODEL), lambda i: (i, 0)),
        out_shape=jax.ShapeDtypeStruct((total, D_MODEL), F32),
        compiler_params=pltpu.CompilerParams(
            dimension_semantics=("arbitrary",),
            vmem_limit_bytes=VMEM_LIMIT_BYTES),
    )(x2, *weights)
    return y.reshape(batch, seq, D_MODEL)


def kernel(x_prompt, x_sample, state_rglru_conv, state_rglru_h, state_mlstm_conv, state_mlstm_C, state_mlstm_n, state_mlstm_m, norm_mix, w_in, b_if, conv_rglru_w, conv_rglru_b, rglru_wa, rglru_ba, rglru_wx, rglru_bx, rglru_lambda, conv_mlstm_w, conv_mlstm_b, mlstm_wq, mlstm_wk, mlstm_wv, w_branch_a, w_branch_b, w_out, norm_mlp, w_mlp_up, w_mlp_down, norm_final):
    depth = w_in.shape[0]
    assert depth == 1, "the final norm is fused into the (single) layer's MLP kernel"
    l = 0
    bp = x_prompt.shape[0]
    row = lambda t: t.reshape(1, -1)
    wt = jnp.swapaxes(w_in[l], 0, 1)
    o_xm = 2 * D_RNN
    o_zm = o_xm + D_ML
    o_i = o_zm + D_ML
    o_f = o_i + H_ML
    o_g = o_f + H_ML
    assert o_xm == D_ML, "the first three column groups of w_in are packed as equal-width blocks"
    lane_pad = lambda t: jnp.pad(t, ((0, 0), (0, LANES - t.shape[1])))
    mixer_w = (
        row(norm_mix[l]),
        _pack_rows_t(wt, 0, D_ML),
        _pack_rows_t(wt, 1, D_ML),
        _pack_rows_t(wt, 2, D_ML),
        _pack_rows_t(wt[o_g:], 0, 2 * D_MODEL),
        _gate_weights(wt, o_i),
        jnp.concatenate([lane_pad(row(b_if[l, :H_ML])), lane_pad(row(b_if[l, H_ML:]))], axis=1),
        conv_rglru_w[l], row(conv_rglru_b[l]),
        _pack_rows(jnp.concatenate([rglru_wa[l], rglru_wx[l]], axis=-1)),
        row(rglru_ba[l]), row(rglru_bx[l]), row(rglru_lambda[l]),
        conv_mlstm_w[l], row(conv_mlstm_b[l]),
        _pack_rows(jnp.concatenate([mlstm_wq[l], mlstm_wk[l]], axis=-1)),
        _pack_rows(mlstm_wv[l]),
        _pack_rows(w_branch_a[l]), _pack_rows(w_branch_b[l]), _pack_rows(w_out[l]),
    )
    outp = _mixer(x_prompt,
                  jnp.zeros((bp, CONV_W - 1, D_RNN), F32), jnp.zeros((bp, D_RNN), F32),
                  jnp.zeros((bp, CONV_W - 1, D_ML), F32), jnp.zeros((bp, H_ML, DV, DK), F32),
                  jnp.zeros((bp, H_ML, DK), F32), jnp.zeros((bp, H_ML), F32),
                  mixer_w, nb=1, ts=256, chunk=256)
    outs = _mixer(x_sample, state_rglru_conv[l], state_rglru_h[l], state_mlstm_conv[l],
                  state_mlstm_C[l], state_mlstm_n[l], state_mlstm_m[l],
                  mixer_w, nb=4, ts=64, chunk=64)
    mlp_w = (row(norm_mlp[l]), _pack_rows(w_mlp_up[l]), _pack_rows(w_mlp_down[l]), row(norm_final))
    y_prompt = _mlp(outp[0], mlp_w, rows=1024)
    y_sample = _mlp(outs[0], mlp_w, rows=1024)
    p_state = [t[None] for t in outp[1:]]
    s_state = [t[None] for t in outs[1:]]
    return (y_prompt, y_sample, *p_state, *s_state)
```
